```python
import math
import jax, jax.numpy as jnp
from jax import lax
import numpy as np

D_MODEL = 1024
BATCH = 4
SEQ = 8192
DEPTH = 2

GRID_W = 64
HEAD_DIM = 64
NA_HEADS = 4
NA_WIN_ROWS = 8
NA_WIN_COLS = 16
SW_HEADS = 8
SW_KV_HEADS = 2
SW_WINDOW = 128
SW_BLOCK = 128
DIFF_HEADS = 4
DIFF_QK_DIM = HEAD_DIM // 2
DIFF_V_DIM = HEAD_DIM
DIFF_BLOCK = 128
D_FF = 4 * D_MODEL
N_BUCKETS = 32
MAX_DISTANCE = 128
LN_EPS = 1e-5
NEG = -1e30

A_W = NA_HEADS * HEAD_DIM
B_Q_W = SW_HEADS * HEAD_DIM
B_KV_W = SW_KV_HEADS * HEAD_DIM
C_QK_W = DIFF_HEADS * 2 * DIFF_QK_DIM
C_V_W = DIFF_HEADS * DIFF_V_DIM
IN_SPLITS = (A_W, A_W, A_W, B_Q_W, B_KV_W, B_KV_W, C_QK_W, C_QK_W, C_V_W)
IS_VALUE = (0, 0, 1, 0, 0, 1, 0, 0, 1)
IN_WIDTH = A_W * 3 + B_Q_W + 2 * B_KV_W + 2 * C_QK_W + C_V_W
MIX_WIDTH = A_W + B_Q_W + C_V_W

kernel_name = "hybrid_natten_swa_diffattn_encoder"


def layer_norm(x, g, b):
    xf = x.astype(jnp.float32)
    mu = jnp.mean(xf, axis=-1, keepdims=True)
    xc = xf - mu
    var = jnp.mean(xc * xc, axis=-1, keepdims=True)
    y = xc * lax.rsqrt(var + LN_EPS) * g.astype(jnp.float32) + b.astype(jnp.float32)
    return y.astype(x.dtype)


def t5_bucket(rel):
    nb = N_BUCKETS // 2
    max_exact = nb // 2
    n = jnp.abs(rel)
    nf = jnp.maximum(n, 1).astype(jnp.float32)
    large = max_exact + (jnp.log(nf / max_exact) / math.log(MAX_DISTANCE / max_exact)
                         * (nb - max_exact)).astype(jnp.int32)
    large = jnp.minimum(large, nb - 1)
    return jnp.where(rel > 0, nb, 0) + jnp.where(n < max_exact, n, large)


def neighbourhood_attention(q, k, v, rpb):
    b, s, h, d = q.shape
    rows = s // GRID_W
    wr = min(NA_WIN_ROWS, rows)
    qg = q.reshape(b, rows, GRID_W, h, d)
    kg = k.reshape(b, rows, GRID_W, h, d)
    vg = v.reshape(b, rows, GRID_W, h, d)
    r = jnp.arange(rows)
    row_start = jnp.clip(r - wr // 2, 0, rows - wr)
    row_idx = row_start[:, None] + jnp.arange(wr)[None, :]
    k_rows = kg[:, row_idx]
    v_rows = vg[:, row_idx]
    c = jnp.arange(GRID_W)
    col_start = jnp.clip(c - NA_WIN_COLS // 2, 0, GRID_W - NA_WIN_COLS)
    col_off = c[None, :] - col_start[:, None]
    col_mask = (col_off >= 0) & (col_off < NA_WIN_COLS)
    dr = row_idx - r[:, None]
    dc = jnp.clip(c[None, :] - c[:, None], -(NA_WIN_COLS - 1), NA_WIN_COLS - 1)
    bias = rpb[:, dr + NA_WIN_ROWS - 1][..., dc + NA_WIN_COLS - 1]
    bias = bias.transpose(0, 1, 3, 2, 4).astype(jnp.float32)
    scores = jnp.einsum('brqhd,brwkhd->bhrqwk', qg, k_rows).astype(jnp.float32) * (d ** -0.5)
    scores = jnp.where(col_mask[:, None, :], scores + bias[None], NEG)
    sh = scores.shape
    p = jax.nn.softmax(scores.reshape(sh[:-2] + (sh[-2] * sh[-1],)), axis=-1).reshape(sh)
    out = jnp.einsum('bhrqwk,brwkhd->brqhd', p.astype(v.dtype), v_rows)
    return out.reshape(b, s, h, d)


def sliding_window_gqa(q, k, v, sink, bias_table):
    b, s, hq, d = q.shape
    hkv = k.shape[2]
    g = hq // hkv
    nblk = s // SW_BLOCK
    pad = ((0, 0), (SW_BLOCK, SW_BLOCK), (0, 0), (0, 0))
    kp = jnp.pad(k, pad).reshape(b, nblk + 2, SW_BLOCK, hkv, d)
    vp = jnp.pad(v, pad).reshape(b, nblk + 2, SW_BLOCK, hkv, d)
    k_band = jnp.concatenate([kp[:, :-2], kp[:, 1:-1], kp[:, 2:]], axis=2)
    v_band = jnp.concatenate([vp[:, :-2], vp[:, 1:-1], vp[:, 2:]], axis=2)
    qb = q.reshape(b, nblk, SW_BLOCK, hkv, g, d)
    scores = jnp.einsum('bnqhgd,bnkhd->bnhgqk', qb, k_band).astype(jnp.float32) * (d ** -0.5)
    qi = jnp.arange(SW_BLOCK)
    kj = jnp.arange(3 * SW_BLOCK) - SW_BLOCK
    rel = kj[None, :] - qi[:, None]
    bias = bias_table[t5_bucket(rel)].astype(jnp.float32)
    bias = bias.transpose(2, 0, 1).reshape(hkv, g, SW_BLOCK, 3 * SW_BLOCK)
    kpos = jnp.arange(nblk)[:, None] * SW_BLOCK + kj[None, :]
    valid = ((jnp.abs(rel) <= SW_WINDOW)[None]
             & ((kpos >= 0) & (kpos < s))[:, None, :])
    scores = jnp.where(valid[None, :, None, None], scores + bias, NEG)
    sink_col = jnp.broadcast_to(sink.astype(jnp.float32).reshape(hkv, g, 1, 1),
                                scores.shape[:-1] + (1,))
    p = jax.nn.softmax(jnp.concatenate([scores, sink_col], axis=-1), axis=-1)[..., :-1]
    out = jnp.einsum('bnhgqk,bnkhd->bnqhgd', p.astype(v.dtype), v_band)
    return out.reshape(b, s, hq, d)


def differential_attention(q, k, v, lam_q, lam_k, subln_g, bias_table, lam_init):
    b, s, h, _, dqk = q.shape
    nblk = s // DIFF_BLOCK
    scale = dqk ** -0.5
    lqf = lam_q.astype(jnp.float32)
    lkf = lam_k.astype(jnp.float32)
    lam = (jnp.exp(jnp.sum(lqf[0] * lkf[0])) - jnp.exp(jnp.sum(lqf[1] * lkf[1]))
           + lam_init)
    kpos = jnp.arange(s)
    qb = jnp.moveaxis(q.reshape(b, nblk, DIFF_BLOCK, h, 2, dqk), 1, 0)

    def block(args):
        q_blk, i = args
        qpos = i * DIFF_BLOCK + jnp.arange(DIFF_BLOCK)
        rel = kpos[None, :] - qpos[:, None]
        bias = jnp.moveaxis(bias_table[t5_bucket(rel)], -1, 0).astype(jnp.float32)
        scores = jnp.einsum('bqhmd,bkhmd->bhmqk', q_blk, k).astype(jnp.float32) * scale
        p = jax.nn.softmax(scores + bias[None, :, None], axis=-1)
        attn = p[:, :, 0] - lam * p[:, :, 1]
        return jnp.einsum('bhqk,bkhd->bqhd', attn.astype(v.dtype), v)

    out = lax.map(block, (qb, jnp.arange(nblk)))
    out = jnp.moveaxis(out, 0, 1).reshape(b, s, h, v.shape[-1])
    of = out.astype(jnp.float32)
    of = of * lax.rsqrt(jnp.mean(of * of, axis=-1, keepdims=True) + LN_EPS)
    of = of * subln_g.astype(jnp.float32) * (1.0 - lam_init)
    return of.astype(v.dtype)


def setup_inputs(seed: int = 0) -> dict:
    key = jax.random.key(seed)
    ks = jax.random.split(key, 20)
    beta = (8 * DEPTH) ** -0.25
    nrm = jax.random.normal
    col_scale = jnp.concatenate([
        jnp.full((w,), beta if isv else 1.0, jnp.float32) for w, isv in zip(IN_SPLITS, IS_VALUE)])
    x = nrm(ks[0], (BATCH, SEQ, D_MODEL), jnp.float32)
    ln_in_g = 1.0 + 0.02 * nrm(ks[1], (D_MODEL,), jnp.float32)
    ln_in_b = 0.02 * nrm(ks[2], (D_MODEL,), jnp.float32)
    t5_table = 0.2 * nrm(ks[3], (N_BUCKETS, SW_HEADS + DIFF_HEADS), jnp.float32)
    w_in = nrm(ks[4], (DEPTH, D_MODEL, IN_WIDTH), jnp.float32) * (D_MODEL ** -0.5) * col_scale
    w_out = nrm(ks[5], (DEPTH, MIX_WIDTH, D_MODEL), jnp.float32) * (MIX_WIDTH ** -0.5) * beta
    na_rpb = 0.2 * nrm(ks[6], (DEPTH, NA_HEADS, 2 * NA_WIN_ROWS - 1, 2 * NA_WIN_COLS - 1), jnp.float32)
    sw_sink = 0.5 * nrm(ks[7], (DEPTH, SW_HEADS), jnp.float32)
    diff_lam_q = 0.1 * nrm(ks[8], (DEPTH, 2, DIFF_QK_DIM), jnp.float32)
    diff_lam_k = 0.1 * nrm(ks[9], (DEPTH, 2, DIFF_QK_DIM), jnp.float32)
    diff_subln_g = 1.0 + 0.02 * nrm(ks[10], (DEPTH, DIFF_V_DIM), jnp.float32)
    ln_mix_g = 1.0 + 0.02 * nrm(ks[11], (DEPTH, D_MODEL), jnp.float32)
    ln_mix_b = 0.02 * nrm(ks[12], (DEPTH, D_MODEL), jnp.float32)
    w_ff1 = nrm(ks[13], (DEPTH, D_MODEL, D_FF), jnp.float32) * (D_MODEL ** -0.5)
    w_ff2 = nrm(ks[14], (DEPTH, D_FF, D_MODEL), jnp.float32) * (D_FF ** -0.5) * beta
    ln_ff_g = 1.0 + 0.02 * nrm(ks[15], (DEPTH, D_MODEL), jnp.float32)
    ln_ff_b = 0.02 * nrm(ks[16], (DEPTH, D_MODEL), jnp.float32)
    return {"x": x, "ln_in_g": ln_in_g, "ln_in_b": ln_in_b, "t5_table": t5_table,
            "w_in": w_in, "w_out": w_out, "na_rpb": na_rpb, "sw_sink": sw_sink,
            "diff_lam_q": diff_lam_q, "diff_lam_k": diff_lam_k, "diff_subln_g": diff_subln_g,
            "ln_mix_g": ln_mix_g, "ln_mix_b": ln_mix_b, "w_ff1": w_ff1, "w_ff2": w_ff2,
            "ln_ff_g": ln_ff_g, "ln_ff_b": ln_ff_b}


def reference(x, ln_in_g, ln_in_b, t5_table, w_in, w_out, na_rpb, sw_sink,
              diff_lam_q, diff_lam_k, diff_subln_g, ln_mix_g, ln_mix_b,
              w_ff1, w_ff2, ln_ff_g, ln_ff_b):
    alpha = (2 * DEPTH) ** 0.25
    b, s, _ = x.shape
    split_points = np.cumsum(IN_SPLITS)[:-1].tolist()
    sw_table = t5_table[:, :SW_HEADS]
    diff_table = t5_table[:, SW_HEADS:]
    x = layer_norm(x, ln_in_g, ln_in_b)
    for l in range(DEPTH):
        lam_init = 0.8 - 0.6 * math.exp(-0.3 * l)
        proj = x @ w_in[l]
        qa, ka, va, qb, kb, vb, qc, kc, vc = jnp.split(proj, split_points, axis=-1)
        oa = neighbourhood_attention(
            qa.reshape(b, s, NA_HEADS, HEAD_DIM), ka.reshape(b, s, NA_HEADS, HEAD_DIM),
            va.reshape(b, s, NA_HEADS, HEAD_DIM), na_rpb[l])
        ob = sliding_window_gqa(
            qb.reshape(b, s, SW_HEADS, HEAD_DIM), kb.reshape(b, s, SW_KV_HEADS, HEAD_DIM),
            vb.reshape(b, s, SW_KV_HEADS, HEAD_DIM), sw_sink[l], sw_table)
        oc = differential_attention(
            qc.reshape(b, s, DIFF_HEADS, 2, DIFF_QK_DIM), kc.reshape(b, s, DIFF_HEADS, 2, DIFF_QK_DIM),
            vc.reshape(b, s, DIFF_HEADS, DIFF_V_DIM), diff_lam_q[l], diff_lam_k[l],
            diff_subln_g[l], diff_table, lam_init)
        mix = jnp.concatenate([oa.reshape(b, s, A_W), ob.reshape(b, s, B_Q_W),
                               oc.reshape(b, s, C_V_W)], axis=-1) @ w_out[l]
        x = layer_norm(alpha * x + mix, ln_mix_g[l], ln_mix_b[l])
        hdn = jax.nn.relu(x @ w_ff1[l])
        x = layer_norm(alpha * x + (hdn * hdn) @ w_ff2[l], ln_ff_g[l], ln_ff_b[l])
    return x
```

```python
import functools
import math

import numpy as np
import jax
import jax.numpy as jnp
from jax import lax
from jax.experimental import pallas as pl
from jax.experimental.pallas import tpu as pltpu

D_MODEL = 1024
DEPTH = 2
GRID_W = 64
HEAD_DIM = 64
NA_HEADS = 4
NA_WIN_ROWS = 8
NA_WIN_COLS = 16
SW_HEADS = 8
SW_KV_HEADS = 2
SW_WINDOW = 128
DIFF_HEADS = 4
DIFF_QK_DIM = HEAD_DIM // 2
DIFF_V_DIM = HEAD_DIM
D_FF = 4 * D_MODEL
N_BUCKETS = 32
MAX_DISTANCE = 128
LN_EPS = 1e-5
NEG = -1e30

A_W = NA_HEADS * HEAD_DIM
B_Q_W = SW_HEADS * HEAD_DIM
B_KV_W = SW_KV_HEADS * HEAD_DIM
C_QK_W = DIFF_HEADS * 2 * DIFF_QK_DIM
C_V_W = DIFF_HEADS * DIFF_V_DIM
IN_SPLITS = (A_W, A_W, A_W, B_Q_W, B_KV_W, B_KV_W, C_QK_W, C_QK_W, C_V_W)
IN_OFFSETS = tuple(int(v) for v in np.cumsum((0,) + IN_SPLITS[:-1]))
IN_WIDTH = sum(IN_SPLITS)
MIX_WIDTH = A_W + B_Q_W + C_V_W

VMEM_LIMIT_BYTES = 56 * 1024 * 1024

ROW_TILE = 512
NA_ROWS_PER_STEP = 4
NA_KEY_ROWS = 12
SW_TQ = 256
SW_TK = SW_TQ + 2 * SW_WINDOW
DF_TQ = 256
DF_TK = 256


def _params(semantics):
    return pltpu.CompilerParams(dimension_semantics=semantics, vmem_limit_bytes=VMEM_LIMIT_BYTES)


def _layer_norm_f32(x, g, b):
    mu = jnp.mean(x, axis=-1, keepdims=True)
    xc = x - mu
    var = jnp.mean(xc * xc, axis=-1, keepdims=True)
    return xc * lax.rsqrt(var + LN_EPS) * g + b


def _t5_bucket_np(rel):
    rel = np.asarray(rel, dtype=np.int64)
    nb = N_BUCKETS // 2
    max_exact = nb // 2
    n = np.abs(rel)
    nn = np.maximum(n, 1)
    floor_log2_sq = np.floor(np.log2((nn * nn).astype(np.float64)) + 1e-9).astype(np.int64)
    large = np.minimum(max_exact + floor_log2_sq - 6, nb - 1)
    return (np.where(rel > 0, nb, 0) + np.where(n < max_exact, n, large)).astype(np.int32)


def _na_index_tables(rows):
    n_steps = rows // NA_ROWS_PER_STEP
    tables = []
    for step in (0, 1, n_steps - 1):
        r0 = step * NA_ROWS_PER_STEP
        base = int(np.clip(r0 - NA_WIN_ROWS // 2, 0, rows - NA_KEY_ROWS))
        r = r0 + np.arange(NA_ROWS_PER_STEP)[:, None, None, None]
        c = np.arange(GRID_W)[None, :, None, None]
        kr = base + np.arange(NA_KEY_ROWS)[None, None, :, None]
        kc = np.arange(GRID_W)[None, None, None, :]
        row_start = np.clip(r - NA_WIN_ROWS // 2, 0, rows - NA_WIN_ROWS)
        col_start = np.clip(c - NA_WIN_COLS // 2, 0, GRID_W - NA_WIN_COLS)
        valid = ((kr >= row_start) & (kr < row_start + NA_WIN_ROWS)
                 & (kc >= col_start) & (kc < col_start + NA_WIN_COLS))
        dr = np.clip(kr - r, -(NA_WIN_ROWS - 1), NA_WIN_ROWS - 1) + NA_WIN_ROWS - 1
        dc = np.clip(kc - c, -(NA_WIN_COLS - 1), NA_WIN_COLS - 1) + NA_WIN_COLS - 1
        idx = np.broadcast_to(dr * (2 * NA_WIN_COLS - 1) + dc, valid.shape)
        nq = NA_ROWS_PER_STEP * GRID_W
        tables.append((idx.reshape(nq, NA_KEY_ROWS * GRID_W), valid.reshape(nq, NA_KEY_ROWS * GRID_W)))
    idx = np.stack([t[0] for t in tables]).astype(np.int32)
    valid = np.stack([t[1] for t in tables])
    return idx, valid


def _sw_index_tables(seq):
    n_steps = seq // SW_TQ
    buckets, valids = [], []
    for step in (0, 1, n_steps - 1):
        qbase = step * SW_TQ
        kstart = int(np.clip(qbase - SW_WINDOW, 0, seq - SW_TK))
        rel = (kstart + np.arange(SW_TK)[None, :]) - (qbase + np.arange(SW_TQ)[:, None])
        buckets.append(_t5_bucket_np(rel))
        valids.append(np.abs(rel) <= SW_WINDOW)
    return np.stack(buckets), np.stack(valids)


def _df_bucket_tiles():
    tiles = []
    for d in (-2, -1, 0, 1, 2):
        rel = d * DF_TK + np.arange(DF_TK)[None, :] - np.arange(DF_TQ)[:, None]
        tiles.append(_t5_bucket_np(rel))
    return np.stack(tiles)


def _inproj_kernel(x_ref, g_ref, b_ref, w_ref, *out_refs, apply_ln):
    x = x_ref[...]
    if apply_ln:
        xn_ref, *proj_refs = out_refs
        x = _layer_norm_f32(x, g_ref[...], b_ref[...])
        xn_ref[...] = x
    else:
        proj_refs = out_refs
    xb = x.astype(jnp.bfloat16)
    for off, width, o_ref in zip(IN_OFFSETS, IN_SPLITS, proj_refs):
        o_ref[...] = jnp.dot(xb, w_ref[:, off:off + width],
                             preferred_element_type=jnp.float32).astype(o_ref.dtype)


def _inproj(x2d, g, b, w_bf16, apply_ln):
    n_tok = x2d.shape[0]
    row = lambda i: (i, 0)
    const = lambda i: (0, 0)
    out_shape = [jax.ShapeDtypeStruct((n_tok, w), jnp.bfloat16) for w in IN_SPLITS]
    out_specs = [pl.BlockSpec((ROW_TILE, w), row) for w in IN_SPLITS]
    if apply_ln:
        out_shape = [jax.ShapeDtypeStruct((n_tok, D_MODEL), jnp.float32)] + out_shape
        out_specs = [pl.BlockSpec((ROW_TILE, D_MODEL), row)] + out_specs
    return pl.pallas_call(
        functools.partial(_inproj_kernel, apply_ln=apply_ln),
        grid=(n_tok // ROW_TILE,),
        in_specs=[pl.BlockSpec((ROW_TILE, D_MODEL), row),
                  pl.BlockSpec((1, D_MODEL), const),
                  pl.BlockSpec((1, D_MODEL), const),
                  pl.BlockSpec((D_MODEL, IN_WIDTH), const)],
        out_specs=out_specs,
        out_shape=out_shape,
        compiler_params=_params(("parallel",)),
        name="inproj_ln" if apply_ln else "inproj",
    )(x2d, g.reshape(1, D_MODEL), b.reshape(1, D_MODEL), w_bf16)


def _na_kernel(q_ref, k_ref, v_ref, bias_ref, o_ref, *, rows):
    i = pl.program_id(1)
    r0 = i * NA_ROWS_PER_STEP
    base = jnp.clip(r0 - NA_WIN_ROWS // 2, 0, rows - NA_KEY_ROWS)
    kstart = pl.multiple_of(base * GRID_W, GRID_W)
    n_keys = NA_KEY_ROWS * GRID_W
    k_win = k_ref[0, pl.ds(kstart, n_keys), :]
    v_win = v_ref[0, pl.ds(kstart, n_keys), :]
    q = q_ref[0]
    scale = HEAD_DIM ** -0.5
    outs = []
    for h in range(NA_HEADS):
        sl = slice(h * HEAD_DIM, (h + 1) * HEAD_DIM)
        s = lax.dot_general(q[:, sl], k_win[:, sl], (((1,), (1,)), ((), ())),
                            preferred_element_type=jnp.float32)
        s = s * scale + bias_ref[0, h]
        m = jnp.max(s, axis=-1, keepdims=True)
        p = jnp.exp(s - m)
        l = jnp.sum(p, axis=-1, keepdims=True)
        o = jnp.dot(p.astype(jnp.bfloat16), v_win[:, sl], preferred_element_type=jnp.float32)
        outs.append(o / l)
    o_ref[0] = jnp.concatenate(outs, axis=-1).astype(o_ref.dtype)


def _na_attention(q, k, v, rpb):
    bsz, seq, _ = q.shape
    rows = seq // GRID_W
    n_steps = rows // NA_ROWS_PER_STEP
    nq = NA_ROWS_PER_STEP * GRID_W
    n_keys = NA_KEY_ROWS * GRID_W
    idx, valid = _na_index_tables(rows)
    flat = rpb.reshape(NA_HEADS, -1).astype(jnp.float32)
    bias = jnp.where(valid[:, None], jnp.take(flat, idx, axis=1).transpose(1, 0, 2, 3), NEG)

    def bias_map(b, i):
        return (jnp.where(i == 0, 0, jnp.where(i == n_steps - 1, 2, 1)), 0, 0, 0)

    return pl.pallas_call(
        functools.partial(_na_kernel, rows=rows),
        grid=(bsz, n_steps),
        in_specs=[pl.BlockSpec((1, nq, A_W), lambda b, i: (b, i, 0)),
                  pl.BlockSpec((1, seq, A_W), lambda b, i: (b, 0, 0)),
                  pl.BlockSpec((1, seq, A_W), lambda b, i: (b, 0, 0)),
                  pl.BlockSpec((1, NA_HEADS, nq, n_keys), bias_map)],
        out_specs=pl.BlockSpec((1, nq, A_W), lambda b, i: (b, i, 0)),
        out_shape=jax.ShapeDtypeStruct((bsz, seq, A_W), jnp.bfloat16),
        compiler_params=_params(("parallel", "arbitrary")),
        name="na_attention",
    )(q, k, v, bias)


def _sw_kernel(sink_ref, q_ref, k_ref, v_ref, bias_ref, o_ref, *, seq):
    i = pl.program_id(1)
    kstart = pl.multiple_of(jnp.clip(i * SW_TQ - SW_WINDOW, 0, seq - SW_TK), SW_WINDOW)
    k_win = k_ref[0, pl.ds(kstart, SW_TK), :]
    v_win = v_ref[0, pl.ds(kstart, SW_TK), :]
    q = q_ref[0]
    scale = HEAD_DIM ** -0.5
    group = SW_HEADS // SW_KV_HEADS
    outs = []
    for h in range(SW_HEADS):
        kv = h // group
        qsl = slice(h * HEAD_DIM, (h + 1) * HEAD_DIM)
        ksl = slice(kv * HEAD_DIM, (kv + 1) * HEAD_DIM)
        s = lax.dot_general(q[:, qsl], k_win[:, ksl], (((1,), (1,)), ((), ())),
                            preferred_element_type=jnp.float32)
        s = s * scale + bias_ref[0, h]
        sink = sink_ref[h]
        m = jnp.maximum(jnp.max(s, axis=-1, keepdims=True), sink)
        p = jnp.exp(s - m)
        l = jnp.sum(p, axis=-1, keepdims=True) + jnp.exp(sink - m)
        o = jnp.dot(p.astype(jnp.bfloat16), v_win[:, ksl], preferred_element_type=jnp.float32)
        outs.append(o / l)
    o_ref[0] = jnp.concatenate(outs, axis=-1).astype(o_ref.dtype)


def _sw_attention(q, k, v, sink, sw_table):
    bsz, seq, _ = q.shape
    n_steps = seq // SW_TQ
    bucket, valid = _sw_index_tables(seq)
    table_t = sw_table.astype(jnp.float32).T
    bias = jnp.where(valid[:, None], jnp.take(table_t, bucket, axis=1).transpose(1, 0, 2, 3), NEG)

    def bias_map(b, i):
        return (jnp.where(i == 0, 0, jnp.where(i == n_steps - 1, 2, 1)), 0, 0, 0)

    return pl.pallas_call(
        functools.partial(_sw_kernel, seq=seq),
        grid=(bsz, n_steps),
        in_specs=[pl.BlockSpec(memory_space=pltpu.SMEM),
                  pl.BlockSpec((1, SW_TQ, B_Q_W), lambda b, i: (b, i, 0)),
                  pl.BlockSpec((1, seq, B_KV_W), lambda b, i: (b, 0, 0)),
                  pl.BlockSpec((1, seq, B_KV_W), lambda b, i: (b, 0, 0)),
                  pl.BlockSpec((1, SW_HEADS, SW_TQ, SW_TK), bias_map)],
        out_specs=pl.BlockSpec((1, SW_TQ, B_Q_W), lambda b, i: (b, i, 0)),
        out_shape=jax.ShapeDtypeStruct((bsz, seq, B_Q_W), jnp.bfloat16),
        compiler_params=_params(("parallel", "arbitrary")),
        name="sw_attention",
    )(sink.astype(jnp.float32), q, k, v, bias)


def _df_kernel(q_ref, k_ref, v_ref, bias_ref, lq_ref, lk_ref, g_ref, o_ref, m_scr, l_scr, acc_scr, *, lam_init):
    j = pl.program_id(2)

    @pl.when(j == 0)
    def _():
        m_scr[...] = jnp.full_like(m_scr, -jnp.inf)
        l_scr[...] = jnp.zeros_like(l_scr)
        acc_scr[...] = jnp.zeros_like(acc_scr)

    q = q_ref[0]
    k = k_ref[0]
    v = v_ref[0]
    scale = DIFF_QK_DIM ** -0.5
    for h in range(DIFF_HEADS):
        bias = bias_ref[0, h]
        vh = v[:, h * DIFF_V_DIM:(h + 1) * DIFF_V_DIM]
        for c in range(2):
            n = 2 * h + c
            sl = slice(n * DIFF_QK_DIM, (n + 1) * DIFF_QK_DIM)
            s = lax.dot_general(q[:, sl], k[:, sl], (((1,), (1,)), ((), ())),
                                preferred_element_type=jnp.float32)
            s = s * scale + bias
            m_prev = m_scr[n]
            m_new = jnp.maximum(m_prev, jnp.max(s, axis=-1, keepdims=True))
            alpha = jnp.exp(m_prev - m_new)
            p = jnp.exp(s - m_new)
            l_scr[n] = alpha * l_scr[n] + jnp.sum(p, axis=-1, keepdims=True)
            acc_scr[n] = alpha * acc_scr[n] + jnp.dot(p.astype(jnp.bfloat16), vh,
                                                      preferred_element_type=jnp.float32)
            m_scr[n] = m_new

    @pl.when(j == pl.num_programs(2) - 1)
    def _():
        lq = lq_ref[...]
        lk = lk_ref[...]
        dots = jnp.sum(lq * lk, axis=-1, keepdims=True)
        lam = jnp.exp(dots[0:1]) - jnp.exp(dots[1:2]) + lam_init
        outs = []
        for h in range(DIFF_HEADS):
            o1 = acc_scr[2 * h] / l_scr[2 * h]
            o2 = acc_scr[2 * h + 1] / l_scr[2 * h + 1]
            of = o1 - lam * o2
            of = of * lax.rsqrt(jnp.mean(of * of, axis=-1, keepdims=True) + LN_EPS)
            outs.append(of * g_ref[...] * (1.0 - lam_init))
        o_ref[0] = jnp.concatenate(outs, axis=-1).astype(o_ref.dtype)


def _df_attention(q, k, v, lam_q, lam_k, subln_g, diff_table, lam_init):
    bsz, seq, _ = q.shape
    nq, nk = seq // DF_TQ, seq // DF_TK
    table_t = diff_table.astype(jnp.float32).T
    bias = jnp.take(table_t, _df_bucket_tiles(), axis=1).transpose(1, 0, 2, 3)
    n_stat = 2 * DIFF_HEADS
    return pl.pallas_call(
        functools.partial(_df_kernel, lam_init=lam_init),
        grid=(bsz, nq, nk),
        in_specs=[pl.BlockSpec((1, DF_TQ, C_QK_W), lambda b, i, j: (b, i, 0)),
                  pl.BlockSpec((1, DF_TK, C_QK_W), lambda b, i, j: (b, j, 0)),
                  pl.BlockSpec((1, DF_TK, C_V_W), lambda b, i, j: (b, j, 0)),
                  pl.BlockSpec((1, DIFF_HEADS, DF_TQ, DF_TK),
                               lambda b, i, j: (jnp.clip(j - i, -2, 2) + 2, 0, 0, 0)),
                  pl.BlockSpec((2, DIFF_QK_DIM), lambda b, i, j: (0, 0)),
                  pl.BlockSpec((2, DIFF_QK_DIM), lambda b, i, j: (0, 0)),
                  pl.BlockSpec((1, DIFF_V_DIM), lambda b, i, j: (0, 0))],
        out_specs=pl.BlockSpec((1, DF_TQ, C_V_W), lambda b, i, j: (b, i, 0)),
        out_shape=jax.ShapeDtypeStruct((bsz, seq, C_V_W), jnp.bfloat16),
        scratch_shapes=[pltpu.VMEM((n_stat, DF_TQ, 1), jnp.float32),
                        pltpu.VMEM((n_stat, DF_TQ, 1), jnp.float32),
                        pltpu.VMEM((n_stat, DF_TQ, DIFF_V_DIM), jnp.float32)],
        compiler_params=_params(("parallel", "parallel", "arbitrary")),
        name="diff_attention",
    )(q, k, v, bias, lam_q.astype(jnp.float32), lam_k.astype(jnp.float32),
      subln_g.astype(jnp.float32).reshape(1, DIFF_V_DIM))


def _outproj_kernel(x_ref, oa_ref, ob_ref, oc_ref, w_ref, g_ref, b_ref, o_ref, *, alpha):
    mix = jnp.dot(oa_ref[...], w_ref[0:A_W, :], preferred_element_type=jnp.float32)
    mix += jnp.dot(ob_ref[...], w_ref[A_W:A_W + B_Q_W, :], preferred_element_type=jnp.float32)
    mix += jnp.dot(oc_ref[...], w_ref[A_W + B_Q_W:MIX_WIDTH, :], preferred_element_type=jnp.float32)
    o_ref[...] = _layer_norm_f32(alpha * x_ref[...] + mix, g_ref[...], b_ref[...])


def _outproj_ln(x2d, oa, ob, oc, w_bf16, g, b, alpha):
    n_tok = x2d.shape[0]
    row = lambda i: (i, 0)
    const = lambda i: (0, 0)
    return pl.pallas_call(
        functools.partial(_outproj_kernel, alpha=alpha),
        grid=(n_tok // ROW_TILE,),
        in_specs=[pl.BlockSpec((ROW_TILE, D_MODEL), row),
                  pl.BlockSpec((ROW_TILE, A_W), row),
                  pl.BlockSpec((ROW_TILE, B_Q_W), row),
                  pl.BlockSpec((ROW_TILE, C_V_W), row),
                  pl.BlockSpec((MIX_WIDTH, D_MODEL), const),
                  pl.BlockSpec((1, D_MODEL), const),
                  pl.BlockSpec((1, D_MODEL), const)],
        out_specs=pl.BlockSpec((ROW_TILE, D_MODEL), row),
        out_shape=jax.ShapeDtypeStruct((n_tok, D_MODEL), jnp.float32),
        compiler_params=_params(("parallel",)),
        name="outproj_ln",
    )(x2d, oa, ob, oc, w_bf16, g.reshape(1, D_MODEL), b.reshape(1, D_MODEL))


FF_CHUNK = 1024


def _ffn_kernel(x_ref, w1_ref, w2_ref, g_ref, b_ref, o_ref, *, alpha):
    x = x_ref[...]
    xb = x.astype(jnp.bfloat16)
    y = jnp.zeros((ROW_TILE, D_MODEL), jnp.float32)
    for c in range(D_FF // FF_CHUNK):
        sl = slice(c * FF_CHUNK, (c + 1) * FF_CHUNK)
        h = jnp.maximum(jnp.dot(xb, w1_ref[:, sl], preferred_element_type=jnp.float32), 0.0)
        y += jnp.dot((h * h).astype(jnp.bfloat16), w2_ref[sl, :], preferred_element_type=jnp.float32)
    o_ref[...] = _layer_norm_f32(alpha * x + y, g_ref[...], b_ref[...])


def _ffn_ln(x2d, w1_bf16, w2_bf16, g, b, alpha):
    n_tok = x2d.shape[0]
    row = lambda i: (i, 0)
    const = lambda i: (0, 0)
    return pl.pallas_call(
        functools.partial(_ffn_kernel, alpha=alpha),
        grid=(n_tok // ROW_TILE,),
        in_specs=[pl.BlockSpec((ROW_TILE, D_MODEL), row),
                  pl.BlockSpec((D_MODEL, D_FF), const, pipeline_mode=pl.Buffered(1)),
                  pl.BlockSpec((D_FF, D_MODEL), const, pipeline_mode=pl.Buffered(1)),
                  pl.BlockSpec((1, D_MODEL), const),
                  pl.BlockSpec((1, D_MODEL), const)],
        out_specs=pl.BlockSpec((ROW_TILE, D_MODEL), row),
        out_shape=jax.ShapeDtypeStruct((n_tok, D_MODEL), jnp.float32),
        compiler_params=_params(("parallel",)),
        name="ffn_ln",
    )(x2d, w1_bf16, w2_bf16, g.reshape(1, D_MODEL), b.reshape(1, D_MODEL))


def kernel(x, ln_in_g, ln_in_b, t5_table, w_in, w_out, na_rpb, sw_sink, diff_lam_q, diff_lam_k, diff_subln_g,
           ln_mix_g, ln_mix_b, w_ff1, w_ff2, ln_ff_g, ln_ff_b):
    alpha = (2 * DEPTH) ** 0.25
    bsz, seq, _ = x.shape
    assert x.shape[-1] == D_MODEL and seq % max(ROW_TILE, SW_TQ, DF_TQ, DF_TK) == 0
    assert (seq // GRID_W) % NA_ROWS_PER_STEP == 0 and seq // GRID_W >= NA_KEY_ROWS
    sw_table = t5_table[:, :SW_HEADS]
    diff_table = t5_table[:, SW_HEADS:]
    w_in_b = w_in.astype(jnp.bfloat16)
    w_out_b = w_out.astype(jnp.bfloat16)
    w_ff1_b = w_ff1.astype(jnp.bfloat16)
    w_ff2_b = w_ff2.astype(jnp.bfloat16)
    x2d = x.reshape(bsz * seq, D_MODEL)
    for l in range(DEPTH):
        lam_init = 0.8 - 0.6 * math.exp(-0.3 * l)
        if l == 0:
            x2d, *proj = _inproj(x2d, ln_in_g, ln_in_b, w_in_b[l], True)
        else:
            proj = _inproj(x2d, ln_in_g, ln_in_b, w_in_b[l], False)
        qa, ka, va, qb, kb, vb, qc, kc, vc = [p.reshape(bsz, seq, -1) for p in proj]
        oa = _na_attention(qa, ka, va, na_rpb[l])
        ob = _sw_attention(qb, kb, vb, sw_sink[l], sw_table)
        oc = _df_attention(qc, kc, vc, diff_lam_q[l], diff_lam_k[l], diff_subln_g[l], diff_table, lam_init)
        x2d = _outproj_ln(x2d, oa.reshape(-1, A_W), ob.reshape(-1, B_Q_W), oc.reshape(-1, C_V_W),
                          w_out_b[l], ln_mix_g[l], ln_mix_b[l], alpha)
        x2d = _ffn_ln(x2d, w_ff1_b[l], w_ff2_b[l], ln_ff_g[l], ln_ff_b[l], alpha)
    return x2d.reshape(bsz, seq, D_MODEL)
```

```python
import functools
import math

import numpy as np
import jax
import jax.numpy as jnp
from jax import lax
from jax.experimental import pallas as pl
from jax.experimental.pallas import tpu as pltpu

D_MODEL = 1024
DEPTH = 2
GRID_W = 64
HEAD_DIM = 64
NA_HEADS = 4
NA_WIN_ROWS = 8
NA_WIN_COLS = 16
SW_HEADS = 8
SW_KV_HEADS = 2
SW_WINDOW = 128
DIFF_HEADS = 4
DIFF_QK_DIM = HEAD_DIM // 2
DIFF_V_DIM = HEAD_DIM
D_FF = 4 * D_MODEL
N_BUCKETS = 32
MAX_DISTANCE = 128
LN_EPS = 1e-5
NEG = -1e30
LOG2E = math.log2(math.e)

A_W = NA_HEADS * HEAD_DIM
B_Q_W = SW_HEADS * HEAD_DIM
B_KV_W = SW_KV_HEADS * HEAD_DIM
C_QK_W = DIFF_HEADS * 2 * DIFF_QK_DIM
C_V_W = DIFF_HEADS * DIFF_V_DIM
IN_SPLITS = (A_W, A_W, A_W, B_Q_W, B_KV_W, B_KV_W, C_QK_W, C_QK_W, C_V_W)
IN_OFFSETS = tuple(int(v) for v in np.cumsum((0,) + IN_SPLITS[:-1]))
IN_WIDTH = sum(IN_SPLITS)
MIX_WIDTH = A_W + B_Q_W + C_V_W
TOKEN_MAJOR_SPLITS = (0, 1, 2, 3, 4, 5, 7)
QC_SPLIT, VC_SPLIT = 6, 8

VMEM_LIMIT_BYTES = 56 * 1024 * 1024

ROW_TILE = 512
NA_ROWS_PER_STEP = 4
NA_KEY_ROWS = 12
SW_TQ = 256
SW_TK = SW_TQ + 2 * SW_WINDOW
DF_TQ = 256
DF_TK = 256
DF_ONES_ROWS = 16
DF_ACC_ROWS = DIFF_V_DIM + DF_ONES_ROWS
DF_NEAR = MAX_DISTANCE // DF_TK + 1


def _params(semantics):
    return pltpu.CompilerParams(dimension_semantics=semantics, vmem_limit_bytes=VMEM_LIMIT_BYTES)


def _layer_norm_f32(x, g, b):
    mu = jnp.mean(x, axis=-1, keepdims=True)
    xc = x - mu
    var = jnp.mean(xc * xc, axis=-1, keepdims=True)
    return xc * lax.rsqrt(var + LN_EPS) * g + b


def _t5_bucket_np(rel):
    rel = np.asarray(rel, dtype=np.int64)
    nb = N_BUCKETS // 2
    max_exact = nb // 2
    n = np.abs(rel)
    nn = np.maximum(n, 1)
    floor_log2_sq = np.floor(np.log2((nn * nn).astype(np.float64)) + 1e-9).astype(np.int64)
    large = np.minimum(max_exact + floor_log2_sq - 6, nb - 1)
    return (np.where(rel > 0, nb, 0) + np.where(n < max_exact, n, large)).astype(np.int32)


def _na_index_tables(rows):
    n_steps = rows // NA_ROWS_PER_STEP
    tables = []
    for step in (0, 1, n_steps - 1):
        r0 = step * NA_ROWS_PER_STEP
        base = int(np.clip(r0 - NA_WIN_ROWS // 2, 0, rows - NA_KEY_ROWS))
        r = r0 + np.arange(NA_ROWS_PER_STEP)[:, None, None, None]
        c = np.arange(GRID_W)[None, :, None, None]
        kr = base + np.arange(NA_KEY_ROWS)[None, None, :, None]
        kc = np.arange(GRID_W)[None, None, None, :]
        row_start = np.clip(r - NA_WIN_ROWS // 2, 0, rows - NA_WIN_ROWS)
        col_start = np.clip(c - NA_WIN_COLS // 2, 0, GRID_W - NA_WIN_COLS)
        valid = ((kr >= row_start) & (kr < row_start + NA_WIN_ROWS)
                 & (kc >= col_start) & (kc < col_start + NA_WIN_COLS))
        dr = np.clip(kr - r, -(NA_WIN_ROWS - 1), NA_WIN_ROWS - 1) + NA_WIN_ROWS - 1
        dc = np.clip(kc - c, -(NA_WIN_COLS - 1), NA_WIN_COLS - 1) + NA_WIN_COLS - 1
        idx = np.broadcast_to(dr * (2 * NA_WIN_COLS - 1) + dc, valid.shape)
        nq = NA_ROWS_PER_STEP * GRID_W
        tables.append((idx.reshape(nq, NA_KEY_ROWS * GRID_W), valid.reshape(nq, NA_KEY_ROWS * GRID_W)))
    idx = np.stack([t[0] for t in tables]).astype(np.int32)
    valid = np.stack([t[1] for t in tables])
    return idx, valid


def _sw_index_tables(seq):
    n_steps = seq // SW_TQ
    buckets, valids = [], []
    for step in (0, 1, n_steps - 1):
        qbase = step * SW_TQ
        kstart = int(np.clip(qbase - SW_WINDOW, 0, seq - SW_TK))
        rel = (kstart + np.arange(SW_TK)[None, :]) - (qbase + np.arange(SW_TQ)[:, None])
        buckets.append(_t5_bucket_np(rel))
        valids.append(np.abs(rel) <= SW_WINDOW)
    return np.stack(buckets), np.stack(valids)


def _df_bucket_tiles():
    tiles = []
    for d in range(-DF_NEAR, DF_NEAR + 1):
        rel = d * DF_TK + np.arange(DF_TK)[:, None] - np.arange(DF_TQ)[None, :]
        tiles.append(_t5_bucket_np(rel))
    return np.stack(tiles)


def _inproj_kernel(x_ref, g_ref, b_ref, w_ref, wq_t_ref, wv_t_ref, *out_refs, apply_ln):
    x = x_ref[...]
    if apply_ln:
        xn_ref, *out_refs = out_refs
        x = _layer_norm_f32(x, g_ref[...], b_ref[...])
        xn_ref[...] = x
    *proj_refs, qc_t_ref, vc_t_ref = out_refs
    xb = x.astype(jnp.bfloat16)
    for split, o_ref in zip(TOKEN_MAJOR_SPLITS, proj_refs):
        off, width = IN_OFFSETS[split], IN_SPLITS[split]
        o_ref[...] = jnp.dot(xb, w_ref[:, off:off + width],
                             preferred_element_type=jnp.float32).astype(o_ref.dtype)
    nt_dims = (((1,), (1,)), ((), ()))
    qc_t = lax.dot_general(wq_t_ref[...], xb, nt_dims, preferred_element_type=jnp.float32)
    qc_t_ref[0] = (qc_t * (DIFF_QK_DIM ** -0.5 * LOG2E)).astype(qc_t_ref.dtype)
    vc_t = lax.dot_general(wv_t_ref[...], xb, nt_dims, preferred_element_type=jnp.float32)
    vc_t_ref[0] = vc_t.astype(vc_t_ref.dtype)


def _inproj(x2d, g, b, w_bf16, seq, apply_ln):
    n_tok = x2d.shape[0]
    bsz = n_tok // seq
    tiles_per_seq = seq // ROW_TILE
    row = lambda i: (i, 0)
    const = lambda i: (0, 0)
    feat_major = lambda i: (i // tiles_per_seq, 0, i % tiles_per_seq)
    out_shape = [jax.ShapeDtypeStruct((n_tok, IN_SPLITS[s]), jnp.bfloat16) for s in TOKEN_MAJOR_SPLITS]
    out_specs = [pl.BlockSpec((ROW_TILE, IN_SPLITS[s]), row) for s in TOKEN_MAJOR_SPLITS]
    for width in (C_QK_W, C_V_W):
        out_shape.append(jax.ShapeDtypeStruct((bsz, width, seq), jnp.bfloat16))
        out_specs.append(pl.BlockSpec((1, width, ROW_TILE), feat_major))
    if apply_ln:
        out_shape = [jax.ShapeDtypeStruct((n_tok, D_MODEL), jnp.float32)] + out_shape
        out_specs = [pl.BlockSpec((ROW_TILE, D_MODEL), row)] + out_specs
    wq_t = w_bf16[:, IN_OFFSETS[QC_SPLIT]:IN_OFFSETS[QC_SPLIT] + C_QK_W].T
    wv_t = w_bf16[:, IN_OFFSETS[VC_SPLIT]:IN_OFFSETS[VC_SPLIT] + C_V_W].T
    return pl.pallas_call(
        functools.partial(_inproj_kernel, apply_ln=apply_ln),
        grid=(n_tok // ROW_TILE,),
        in_specs=[pl.BlockSpec((ROW_TILE, D_MODEL), row),
                  pl.BlockSpec((1, D_MODEL), const),
                  pl.BlockSpec((1, D_MODEL), const),
                  pl.BlockSpec((D_MODEL, IN_WIDTH), const),
                  pl.BlockSpec((C_QK_W, D_MODEL), const),
                  pl.BlockSpec((C_V_W, D_MODEL), const)],
        out_specs=out_specs,
        out_shape=out_shape,
        compiler_params=_params(("parallel",)),
        name="inproj_ln" if apply_ln else "inproj",
    )(x2d, g.reshape(1, D_MODEL), b.reshape(1, D_MODEL), w_bf16, wq_t, wv_t)


def _na_kernel(q_ref, k_ref, v_ref, bias_ref, o_ref, *, rows):
    i = pl.program_id(1)
    r0 = i * NA_ROWS_PER_STEP
    base = jnp.clip(r0 - NA_WIN_ROWS // 2, 0, rows - NA_KEY_ROWS)
    kstart = pl.multiple_of(base * GRID_W, GRID_W)
    n_keys = NA_KEY_ROWS * GRID_W
    k_win = k_ref[0, pl.ds(kstart, n_keys), :]
    v_win = v_ref[0, pl.ds(kstart, n_keys), :]
    q = q_ref[0]
    scale = HEAD_DIM ** -0.5
    outs = []
    for h in range(NA_HEADS):
        sl = slice(h * HEAD_DIM, (h + 1) * HEAD_DIM)
        s = lax.dot_general(q[:, sl], k_win[:, sl], (((1,), (1,)), ((), ())),
                            preferred_element_type=jnp.float32)
        s = s * scale + bias_ref[0, h]
        m = jnp.max(s, axis=-1, keepdims=True)
        p = jnp.exp(s - m)
        l = jnp.sum(p, axis=-1, keepdims=True)
        o = jnp.dot(p.astype(jnp.bfloat16), v_win[:, sl], preferred_element_type=jnp.float32)
        outs.append(o / l)
    o_ref[0] = jnp.concatenate(outs, axis=-1).astype(o_ref.dtype)


def _na_attention(q, k, v, rpb):
    bsz, seq, _ = q.shape
    rows = seq // GRID_W
    n_steps = rows // NA_ROWS_PER_STEP
    nq = NA_ROWS_PER_STEP * GRID_W
    n_keys = NA_KEY_ROWS * GRID_W
    idx, valid = _na_index_tables(rows)
    flat = rpb.reshape(NA_HEADS, -1).astype(jnp.float32)
    bias = jnp.where(valid[:, None], jnp.take(flat, idx, axis=1).transpose(1, 0, 2, 3), NEG)

    def bias_map(b, i):
        return (jnp.where(i == 0, 0, jnp.where(i == n_steps - 1, 2, 1)), 0, 0, 0)

    return pl.pallas_call(
        functools.partial(_na_kernel, rows=rows),
        grid=(bsz, n_steps),
        in_specs=[pl.BlockSpec((1, nq, A_W), lambda b, i: (b, i, 0)),
                  pl.BlockSpec((1, seq, A_W), lambda b, i: (b, 0, 0)),
                  pl.BlockSpec((1, seq, A_W), lambda b, i: (b, 0, 0)),
                  pl.BlockSpec((1, NA_HEADS, nq, n_keys), bias_map)],
        out_specs=pl.BlockSpec((1, nq, A_W), lambda b, i: (b, i, 0)),
        out_shape=jax.ShapeDtypeStruct((bsz, seq, A_W), jnp.bfloat16),
        compiler_params=_params(("parallel", "arbitrary")),
        name="na_attention",
    )(q, k, v, bias)


def _sw_kernel(sink_ref, q_ref, k_ref, v_ref, bias_ref, o_ref, *, seq):
    i = pl.program_id(1)
    kstart = pl.multiple_of(jnp.clip(i * SW_TQ - SW_WINDOW, 0, seq - SW_TK), SW_WINDOW)
    k_win = k_ref[0, pl.ds(kstart, SW_TK), :]
    v_win = v_ref[0, pl.ds(kstart, SW_TK), :]
    q = q_ref[0]
    scale = HEAD_DIM ** -0.5
    group = SW_HEADS // SW_KV_HEADS
    outs = []
    for h in range(SW_HEADS):
        kv = h // group
        qsl = slice(h * HEAD_DIM, (h + 1) * HEAD_DIM)
        ksl = slice(kv * HEAD_DIM, (kv + 1) * HEAD_DIM)
        s = lax.dot_general(q[:, qsl], k_win[:, ksl], (((1,), (1,)), ((), ())),
                            preferred_element_type=jnp.float32)
        s = s * scale + bias_ref[0, h]
        sink = sink_ref[h]
        m = jnp.maximum(jnp.max(s, axis=-1, keepdims=True), sink)
        p = jnp.exp(s - m)
        l = jnp.sum(p, axis=-1, keepdims=True) + jnp.exp(sink - m)
        o = jnp.dot(p.astype(jnp.bfloat16), v_win[:, ksl], preferred_element_type=jnp.float32)
        outs.append(o / l)
    o_ref[0] = jnp.concatenate(outs, axis=-1).astype(o_ref.dtype)


def _sw_attention(q, k, v, sink, sw_table):
    bsz, seq, _ = q.shape
    n_steps = seq // SW_TQ
    bucket, valid = _sw_index_tables(seq)
    table_t = sw_table.astype(jnp.float32).T
    bias = jnp.where(valid[:, None], jnp.take(table_t, bucket, axis=1).transpose(1, 0, 2, 3), NEG)

    def bias_map(b, i):
        return (jnp.where(i == 0, 0, jnp.where(i == n_steps - 1, 2, 1)), 0, 0, 0)

    return pl.pallas_call(
        functools.partial(_sw_kernel, seq=seq),
        grid=(bsz, n_steps),
        in_specs=[pl.BlockSpec(memory_space=pltpu.SMEM),
                  pl.BlockSpec((1, SW_TQ, B_Q_W), lambda b, i: (b, i, 0)),
                  pl.BlockSpec((1, seq, B_KV_W), lambda b, i: (b, 0, 0)),
                  pl.BlockSpec((1, seq, B_KV_W), lambda b, i: (b, 0, 0)),
                  pl.BlockSpec((1, SW_HEADS, SW_TQ, SW_TK), bias_map)],
        out_specs=pl.BlockSpec((1, SW_TQ, B_Q_W), lambda b, i: (b, i, 0)),
        out_shape=jax.ShapeDtypeStruct((bsz, seq, B_Q_W), jnp.bfloat16),
        compiler_params=_params(("parallel", "arbitrary")),
        name="sw_attention",
    )(sink.astype(jnp.float32), q, k, v, bias)


def _df_kernel(cfar_ref, q_t_ref, k_ref, v_t_ref, bias_ref, lq_ref, lk_ref, g_ref, o_ref,
               qm_scr, m_scr, acc_scr, *, lam_init, n_chunks):
    i = pl.program_id(1)
    n_stat = 2 * DIFF_HEADS

    q_t = q_t_ref[0]
    zeros = jnp.zeros((C_QK_W, DF_TQ), jnp.bfloat16)
    for n in range(n_stat):
        rows = slice(n * DIFF_QK_DIM, (n + 1) * DIFF_QK_DIM)
        qm_scr[n] = zeros
        qm_scr[n, rows, :] = q_t[rows, :]
    m_scr[...] = jnp.full_like(m_scr, NEG)
    acc_scr[...] = jnp.zeros_like(acc_scr)
    ones = jnp.ones((DF_ONES_ROWS, DF_TK), jnp.bfloat16)

    def chunk(j, near_tile=None, far_side=None):
        kstart = pl.multiple_of(j * DF_TK, DF_TK)
        k_chunk = k_ref[0, pl.ds(kstart, DF_TK), :]

        def scores(n):
            return jnp.dot(k_chunk, qm_scr[n], preferred_element_type=jnp.float32)

        s_next = scores(0)
        for n in range(n_stat):
            h = n // 2
            s = s_next
            if n + 1 < n_stat:
                s_next = scores(n + 1)
            v_aug = jnp.concatenate(
                [v_t_ref[0, h * DIFF_V_DIM:(h + 1) * DIFF_V_DIM, pl.ds(kstart, DF_TK)], ones], axis=0)
            m_prev = m_scr[n]
            if near_tile is not None:
                s = s + bias_ref[near_tile, h]
                m_new = jnp.maximum(m_prev, jnp.max(s, axis=0, keepdims=True))
                shift = m_new
            else:
                cb = cfar_ref[far_side, h]
                m_new = jnp.maximum(m_prev, jnp.max(s, axis=0, keepdims=True) + cb)
                shift = m_new - cb
            alpha = jnp.exp2(m_prev - m_new)
            p = jnp.exp2(s - shift).astype(jnp.bfloat16)
            acc_scr[n] = alpha * acc_scr[n] + jnp.dot(v_aug, p, preferred_element_type=jnp.float32)
            m_scr[n] = m_new

    def far_loop(lo, hi, side):
        def body(j, carry):
            chunk(j, far_side=side)
            return carry
        lax.fori_loop(lo, hi, body, 0)

    far_loop(0, jnp.maximum(i - DF_NEAR, 0), 0)
    for t, dj in enumerate(range(-DF_NEAR, DF_NEAR + 1)):
        j = i + dj

        @pl.when((j >= 0) & (j < n_chunks))
        def _():
            chunk(j, near_tile=t)
    far_loop(jnp.minimum(i + DF_NEAR + 1, n_chunks), n_chunks, 1)

    dots = jnp.sum(lq_ref[...] * lk_ref[...], axis=-1, keepdims=True)
    lam = jnp.exp(dots[0:1]) - jnp.exp(dots[1:2]) + lam_init
    outs = []
    for h in range(DIFF_HEADS):
        a1 = acc_scr[2 * h]
        a2 = acc_scr[2 * h + 1]
        o1 = a1[:DIFF_V_DIM] / a1[DIFF_V_DIM:DIFF_V_DIM + 1]
        o2 = a2[:DIFF_V_DIM] / a2[DIFF_V_DIM:DIFF_V_DIM + 1]
        of = o1 - lam * o2
        of = of * lax.rsqrt(jnp.mean(of * of, axis=0, keepdims=True) + LN_EPS)
        outs.append(of * g_ref[...] * (1.0 - lam_init))
    o_ref[0] = jnp.concatenate(outs, axis=0).T.astype(o_ref.dtype)


def _df_attention(q_t, k, v_t, lam_q, lam_k, subln_g, diff_table, lam_init):
    bsz, seq, _ = k.shape
    n_chunks = seq // DF_TK
    table_t = diff_table.astype(jnp.float32).T * LOG2E
    bias = jnp.take(table_t, _df_bucket_tiles(), axis=1).transpose(1, 0, 2, 3)
    nb = N_BUCKETS // 2
    cfar = jnp.stack([table_t[:, nb - 1], table_t[:, N_BUCKETS - 1]])
    n_stat = 2 * DIFF_HEADS
    n_tiles = 2 * DF_NEAR + 1
    const2 = lambda b, i: (0, 0)
    return pl.pallas_call(
        functools.partial(_df_kernel, lam_init=lam_init, n_chunks=n_chunks),
        grid=(bsz, seq // DF_TQ),
        in_specs=[pl.BlockSpec(memory_space=pltpu.SMEM),
                  pl.BlockSpec((1, C_QK_W, DF_TQ), lambda b, i: (b, 0, i)),
                  pl.BlockSpec((1, seq, C_QK_W), lambda b, i: (b, 0, 0)),
                  pl.BlockSpec((1, C_V_W, seq), lambda b, i: (b, 0, 0)),
                  pl.BlockSpec((n_tiles, DIFF_HEADS, DF_TK, DF_TQ), lambda b, i: (0, 0, 0, 0)),
                  pl.BlockSpec((2, DIFF_QK_DIM), const2),
                  pl.BlockSpec((2, DIFF_QK_DIM), const2),
                  pl.BlockSpec((DIFF_V_DIM, 1), const2)],
        out_specs=pl.BlockSpec((1, DF_TQ, C_V_W), lambda b, i: (b, i, 0)),
        out_shape=jax.ShapeDtypeStruct((bsz, seq, C_V_W), jnp.bfloat16),
        scratch_shapes=[pltpu.VMEM((n_stat, C_QK_W, DF_TQ), jnp.bfloat16),
                        pltpu.VMEM((n_stat, 1, DF_TQ), jnp.float32),
                        pltpu.VMEM((n_stat, DF_ACC_ROWS, DF_TQ), jnp.float32)],
        compiler_params=_params(("parallel", "arbitrary")),
        name="diff_attention",
    )(cfar, q_t, k, v_t, bias, lam_q.astype(jnp.float32), lam_k.astype(jnp.float32),
      subln_g.astype(jnp.float32).reshape(DIFF_V_DIM, 1))


def _outproj_kernel(x_ref, oa_ref, ob_ref, oc_ref, w_ref, g_ref, b_ref, o_ref, *, alpha):
    mix = jnp.dot(oa_ref[...], w_ref[0:A_W, :], preferred_element_type=jnp.float32)
    mix += jnp.dot(ob_ref[...], w_ref[A_W:A_W + B_Q_W, :], preferred_element_type=jnp.float32)
    mix += jnp.dot(oc_ref[...], w_ref[A_W + B_Q_W:MIX_WIDTH, :], preferred_element_type=jnp.float32)
    o_ref[...] = _layer_norm_f32(alpha * x_ref[...] + mix, g_ref[...], b_ref[...])


def _outproj_ln(x2d, oa, ob, oc, w_bf16, g, b, alpha):
    n_tok = x2d.shape[0]
    row = lambda i: (i, 0)
    const = lambda i: (0, 0)
    return pl.pallas_call(
        functools.partial(_outproj_kernel, alpha=alpha),
        grid=(n_tok // ROW_TILE,),
        in_specs=[pl.BlockSpec((ROW_TILE, D_MODEL), row),
                  pl.BlockSpec((ROW_TILE, A_W), row),
                  pl.BlockSpec((ROW_TILE, B_Q_W), row),
                  pl.BlockSpec((ROW_TILE, C_V_W), row),
                  pl.BlockSpec((MIX_WIDTH, D_MODEL), const),
                  pl.BlockSpec((1, D_MODEL), const),
                  pl.BlockSpec((1, D_MODEL), const)],
        out_specs=pl.BlockSpec((ROW_TILE, D_MODEL), row),
        out_shape=jax.ShapeDtypeStruct((n_tok, D_MODEL), jnp.float32),
        compiler_params=_params(("parallel",)),
        name="outproj_ln",
    )(x2d, oa, ob, oc, w_bf16, g.reshape(1, D_MODEL), b.reshape(1, D_MODEL))


FF_CHUNK = 1024


def _ffn_kernel(x_ref, w1_ref, w2_ref, g_ref, b_ref, o_ref, *, alpha):
    x = x_ref[...]
    xb = x.astype(jnp.bfloat16)
    y = jnp.zeros((ROW_TILE, D_MODEL), jnp.float32)
    for c in range(D_FF // FF_CHUNK):
        sl = slice(c * FF_CHUNK, (c + 1) * FF_CHUNK)
        h = jnp.maximum(jnp.dot(xb, w1_ref[:, sl], preferred_element_type=jnp.float32), 0.0)
        y += jnp.dot((h * h).astype(jnp.bfloat16), w2_ref[sl, :], preferred_element_type=jnp.float32)
    o_ref[...] = _layer_norm_f32(alpha * x + y, g_ref[...], b_ref[...])


def _ffn_ln(x2d, w1_bf16, w2_bf16, g, b, alpha):
    n_tok = x2d.shape[0]
    row = lambda i: (i, 0)
    const = lambda i: (0, 0)
    return pl.pallas_call(
        functools.partial(_ffn_kernel, alpha=alpha),
        grid=(n_tok // ROW_TILE,),
        in_specs=[pl.BlockSpec((ROW_TILE, D_MODEL), row),
                  pl.BlockSpec((D_MODEL, D_FF), const, pipeline_mode=pl.Buffered(1)),
                  pl.BlockSpec((D_FF, D_MODEL), const, pipeline_mode=pl.Buffered(1)),
                  pl.BlockSpec((1, D_MODEL), const),
                  pl.BlockSpec((1, D_MODEL), const)],
        out_specs=pl.BlockSpec((ROW_TILE, D_MODEL), row),
        out_shape=jax.ShapeDtypeStruct((n_tok, D_MODEL), jnp.float32),
        compiler_params=_params(("parallel",)),
        name="ffn_ln",
    )(x2d, w1_bf16, w2_bf16, g.reshape(1, D_MODEL), b.reshape(1, D_MODEL))


def kernel(x, ln_in_g, ln_in_b, t5_table, w_in, w_out, na_rpb, sw_sink, diff_lam_q, diff_lam_k, diff_subln_g,
           ln_mix_g, ln_mix_b, w_ff1, w_ff2, ln_ff_g, ln_ff_b):
    alpha = (2 * DEPTH) ** 0.25
    bsz, seq, _ = x.shape
    assert x.shape[-1] == D_MODEL and seq % max(ROW_TILE, SW_TQ, DF_TQ, DF_TK) == 0
    assert (seq // GRID_W) % NA_ROWS_PER_STEP == 0 and seq // GRID_W >= NA_KEY_ROWS
    sw_table = t5_table[:, :SW_HEADS]
    diff_table = t5_table[:, SW_HEADS:]
    w_in_b = w_in.astype(jnp.bfloat16)
    w_out_b = w_out.astype(jnp.bfloat16)
    w_ff1_b = w_ff1.astype(jnp.bfloat16)
    w_ff2_b = w_ff2.astype(jnp.bfloat16)
    x2d = x.reshape(bsz * seq, D_MODEL)
    for l in range(DEPTH):
        lam_init = 0.8 - 0.6 * math.exp(-0.3 * l)
        if l == 0:
            x2d, *proj = _inproj(x2d, ln_in_g, ln_in_b, w_in_b[l], seq, True)
        else:
            proj = _inproj(x2d, ln_in_g, ln_in_b, w_in_b[l], seq, False)
        *tok_major, qc_t, vc_t = proj
        qa, ka, va, qb, kb, vb, kc = [p.reshape(bsz, seq, -1) for p in tok_major]
        oa = _na_attention(qa, ka, va, na_rpb[l])
        ob = _sw_attention(qb, kb, vb, sw_sink[l], sw_table)
        oc = _df_attention(qc_t, kc, vc_t, diff_lam_q[l], diff_lam_k[l], diff_subln_g[l], diff_table, lam_init)
        x2d = _outproj_ln(x2d, oa.reshape(-1, A_W), ob.reshape(-1, B_Q_W), oc.reshape(-1, C_V_W),
                          w_out_b[l], ln_mix_g[l], ln_mix_b[l], alpha)
        x2d = _ffn_ln(x2d, w_ff1_b[l], w_ff2_b[l], ln_ff_g[l], ln_ff_b[l], alpha)
    return x2d.reshape(bsz, seq, D_MODEL)
```

```python
import functools
import math

import numpy as np
import jax
import jax.numpy as jnp
from jax import lax
from jax.experimental import pallas as pl
from jax.experimental.pallas import tpu as pltpu

D_MODEL = 1024
DEPTH = 2
GRID_W = 64
HEAD_DIM = 64
NA_HEADS = 4
NA_WIN_ROWS = 8
NA_WIN_COLS = 16
SW_HEADS = 8
SW_KV_HEADS = 2
SW_WINDOW = 128
DIFF_HEADS = 4
DIFF_QK_DIM = HEAD_DIM // 2
DIFF_V_DIM = HEAD_DIM
D_FF = 4 * D_MODEL
N_BUCKETS = 32
MAX_DISTANCE = 128
LN_EPS = 1e-5
NEG = -1e30
LOG2E = math.log2(math.e)

A_W = NA_HEADS * HEAD_DIM
B_Q_W = SW_HEADS * HEAD_DIM
B_KV_W = SW_KV_HEADS * HEAD_DIM
C_QK_W = DIFF_HEADS * 2 * DIFF_QK_DIM
C_V_W = DIFF_HEADS * DIFF_V_DIM
IN_SPLITS = (A_W, A_W, A_W, B_Q_W, B_KV_W, B_KV_W, C_QK_W, C_QK_W, C_V_W)
IN_OFFSETS = tuple(int(v) for v in np.cumsum((0,) + IN_SPLITS[:-1]))
IN_WIDTH = sum(IN_SPLITS)
MIX_WIDTH = A_W + B_Q_W + C_V_W
TOKEN_MAJOR_SPLITS = (0, 1, 2, 3, 4, 5, 7)
QC_SPLIT, VC_SPLIT = 6, 8

VMEM_LIMIT_BYTES = 56 * 1024 * 1024

ROW_TILE = 512
NA_ROWS_PER_STEP = 4
NA_KEY_ROWS = 12
SW_TQ = 256
SW_TK = SW_TQ + 2 * SW_WINDOW
DF_TQ = 256
DF_TK = 256
DF_ONES_ROWS = 16
DF_ACC_ROWS = DIFF_V_DIM + DF_ONES_ROWS
DF_NEAR = MAX_DISTANCE // DF_TK + 1
DF_LOOKAHEAD = 5
BIAS_STRIP = 8


def _params(semantics):
    return pltpu.CompilerParams(dimension_semantics=semantics, vmem_limit_bytes=VMEM_LIMIT_BYTES)


def _layer_norm_f32(x, g, b):
    mu = jnp.mean(x, axis=-1, keepdims=True)
    xc = x - mu
    var = jnp.mean(xc * xc, axis=-1, keepdims=True)
    return xc * lax.rsqrt(var + LN_EPS) * g + b


def _t5_bucket_np(rel):
    rel = np.asarray(rel, dtype=np.int64)
    nb = N_BUCKETS // 2
    max_exact = nb // 2
    n = np.abs(rel)
    nn = np.maximum(n, 1)
    floor_log2_sq = np.floor(np.log2((nn * nn).astype(np.float64)) + 1e-9).astype(np.int64)
    large = np.minimum(max_exact + floor_log2_sq - 6, nb - 1)
    return (np.where(rel > 0, nb, 0) + np.where(n < max_exact, n, large)).astype(np.int32)


def _na_index_tables(rows):
    n_steps = rows // NA_ROWS_PER_STEP
    tables = []
    for step in (0, 1, n_steps - 1):
        r0 = step * NA_ROWS_PER_STEP
        base = int(np.clip(r0 - NA_WIN_ROWS // 2, 0, rows - NA_KEY_ROWS))
        r = r0 + np.arange(NA_ROWS_PER_STEP)[:, None, None, None]
        c = np.arange(GRID_W)[None, :, None, None]
        kr = base + np.arange(NA_KEY_ROWS)[None, None, :, None]
        kc = np.arange(GRID_W)[None, None, None, :]
        row_start = np.clip(r - NA_WIN_ROWS // 2, 0, rows - NA_WIN_ROWS)
        col_start = np.clip(c - NA_WIN_COLS // 2, 0, GRID_W - NA_WIN_COLS)
        valid = ((kr >= row_start) & (kr < row_start + NA_WIN_ROWS)
                 & (kc >= col_start) & (kc < col_start + NA_WIN_COLS))
        dr = np.clip(kr - r, -(NA_WIN_ROWS - 1), NA_WIN_ROWS - 1) + NA_WIN_ROWS - 1
        dc = np.clip(kc - c, -(NA_WIN_COLS - 1), NA_WIN_COLS - 1) + NA_WIN_COLS - 1
        idx = np.broadcast_to(dr * (2 * NA_WIN_COLS - 1) + dc, valid.shape)
        nq = NA_ROWS_PER_STEP * GRID_W
        tables.append((idx.reshape(nq, NA_KEY_ROWS * GRID_W), valid.reshape(nq, NA_KEY_ROWS * GRID_W)))
    idx = np.stack([t[0] for t in tables]).astype(np.int32)
    valid = np.stack([t[1] for t in tables])
    return np.where(valid, idx, -1).astype(np.int32)


def _sw_index_tables(seq):
    n_steps = seq // SW_TQ
    buckets = []
    for step in (0, 1, n_steps - 1):
        qbase = step * SW_TQ
        kstart = int(np.clip(qbase - SW_WINDOW, 0, seq - SW_TK))
        rel = (kstart + np.arange(SW_TK)[None, :]) - (qbase + np.arange(SW_TQ)[:, None])
        buckets.append(np.where(np.abs(rel) <= SW_WINDOW, _t5_bucket_np(rel), -1))
    return np.stack(buckets).astype(np.int32)


def _df_bucket_tiles():
    tiles = []
    for d in range(-DF_NEAR - 1, DF_NEAR + 2):
        rel = d * DF_TK + np.arange(DF_TK)[:, None] - np.arange(DF_TQ)[None, :]
        tiles.append(_t5_bucket_np(rel))
    tiles = np.stack(tiles)
    assert (tiles[0] == tiles[0, 0, 0]).all() and (tiles[-1] == tiles[-1, 0, 0]).all()
    return tiles


def _bias_lookup_kernel(table_ref, idx_ref, o_ref, *, n_entries):
    h = pl.program_id(1)

    def strip(r, carry):
        r0 = pl.multiple_of(r * BIAS_STRIP, BIAS_STRIP)
        idx = idx_ref[0, pl.ds(r0, BIAS_STRIP), :]
        acc = jnp.full(idx.shape, NEG, jnp.float32)
        for e in range(n_entries):
            acc = jnp.where(idx == e, table_ref[h, e], acc)
        o_ref[0, 0, pl.ds(r0, BIAS_STRIP), :] = acc
        return carry

    lax.fori_loop(0, idx_ref.shape[1] // BIAS_STRIP, strip, 0)


def _bias_lookup(table, idx, name):
    n_heads, n_entries = table.shape
    n_tiles, rows, cols = idx.shape
    return pl.pallas_call(
        functools.partial(_bias_lookup_kernel, n_entries=n_entries),
        grid=(n_tiles, n_heads),
        in_specs=[pl.BlockSpec(memory_space=pltpu.SMEM),
                  pl.BlockSpec((1, rows, cols), lambda t, h: (t, 0, 0))],
        out_specs=pl.BlockSpec((1, 1, rows, cols), lambda t, h: (t, h, 0, 0)),
        out_shape=jax.ShapeDtypeStruct((n_tiles, n_heads, rows, cols), jnp.float32),
        compiler_params=_params(("arbitrary", "arbitrary")),
        name=name,
    )(table.astype(jnp.float32), jnp.asarray(idx))


def _inproj_kernel(x_ref, g_ref, b_ref, w_ref, wq_t_ref, wv_t_ref, *out_refs, apply_ln):
    x = x_ref[...]
    if apply_ln:
        xn_ref, *out_refs = out_refs
        x = _layer_norm_f32(x, g_ref[...], b_ref[...])
        xn_ref[...] = x
    *proj_refs, qc_t_ref, vc_t_ref = out_refs
    xb = x.astype(jnp.bfloat16)
    for split, o_ref in zip(TOKEN_MAJOR_SPLITS, proj_refs):
        off, width = IN_OFFSETS[split], IN_SPLITS[split]
        o_ref[...] = jnp.dot(xb, w_ref[:, off:off + width],
                             preferred_element_type=jnp.float32).astype(o_ref.dtype)
    nt_dims = (((1,), (1,)), ((), ()))
    qc_t = lax.dot_general(wq_t_ref[...], xb, nt_dims, preferred_element_type=jnp.float32)
    qc_t_ref[0] = (qc_t * (DIFF_QK_DIM ** -0.5 * LOG2E)).astype(qc_t_ref.dtype)
    vc_t = lax.dot_general(wv_t_ref[...], xb, nt_dims, preferred_element_type=jnp.float32)
    vc_t_ref[0] = vc_t.astype(vc_t_ref.dtype)


def _inproj(x2d, g, b, w_bf16, seq, apply_ln):
    n_tok = x2d.shape[0]
    bsz = n_tok // seq
    tiles_per_seq = seq // ROW_TILE
    row = lambda i: (i, 0)
    const = lambda i: (0, 0)
    feat_major = lambda i: (i // tiles_per_seq, 0, i % tiles_per_seq)
    out_shape = [jax.ShapeDtypeStruct((n_tok, IN_SPLITS[s]), jnp.bfloat16) for s in TOKEN_MAJOR_SPLITS]
    out_specs = [pl.BlockSpec((ROW_TILE, IN_SPLITS[s]), row) for s in TOKEN_MAJOR_SPLITS]
    for width in (C_QK_W, C_V_W):
        out_shape.append(jax.ShapeDtypeStruct((bsz, width, seq), jnp.bfloat16))
        out_specs.append(pl.BlockSpec((1, width, ROW_TILE), feat_major))
    if apply_ln:
        out_shape = [jax.ShapeDtypeStruct((n_tok, D_MODEL), jnp.float32)] + out_shape
        out_specs = [pl.BlockSpec((ROW_TILE, D_MODEL), row)] + out_specs
    wq_t = w_bf16[:, IN_OFFSETS[QC_SPLIT]:IN_OFFSETS[QC_SPLIT] + C_QK_W].T
    wv_t = w_bf16[:, IN_OFFSETS[VC_SPLIT]:IN_OFFSETS[VC_SPLIT] + C_V_W].T
    return pl.pallas_call(
        functools.partial(_inproj_kernel, apply_ln=apply_ln),
        grid=(n_tok // ROW_TILE,),
        in_specs=[pl.BlockSpec((ROW_TILE, D_MODEL), row),
                  pl.BlockSpec((1, D_MODEL), const),
                  pl.BlockSpec((1, D_MODEL), const),
                  pl.BlockSpec((D_MODEL, IN_WIDTH), const),
                  pl.BlockSpec((C_QK_W, D_MODEL), const),
                  pl.BlockSpec((C_V_W, D_MODEL), const)],
        out_specs=out_specs,
        out_shape=out_shape,
        compiler_params=_params(("parallel",)),
        name="inproj_ln" if apply_ln else "inproj",
    )(x2d, g.reshape(1, D_MODEL), b.reshape(1, D_MODEL), w_bf16, wq_t, wv_t)


def _na_kernel(q_ref, k_ref, v_ref, bias_ref, o_ref, *, rows):
    i = pl.program_id(1)
    r0 = i * NA_ROWS_PER_STEP
    base = jnp.clip(r0 - NA_WIN_ROWS // 2, 0, rows - NA_KEY_ROWS)
    kstart = pl.multiple_of(base * GRID_W, GRID_W)
    n_keys = NA_KEY_ROWS * GRID_W
    k_win = k_ref[0, pl.ds(kstart, n_keys), :]
    v_win = v_ref[0, pl.ds(kstart, n_keys), :]
    q = q_ref[0]
    scale = HEAD_DIM ** -0.5
    outs = []
    for h in range(NA_HEADS):
        sl = slice(h * HEAD_DIM, (h + 1) * HEAD_DIM)
        s = lax.dot_general(q[:, sl], k_win[:, sl], (((1,), (1,)), ((), ())),
                            preferred_element_type=jnp.float32)
        s = s * scale + bias_ref[0, h]
        m = jnp.max(s, axis=-1, keepdims=True)
        p = jnp.exp(s - m)
        l = jnp.sum(p, axis=-1, keepdims=True)
        o = jnp.dot(p.astype(jnp.bfloat16), v_win[:, sl], preferred_element_type=jnp.float32)
        outs.append(o / l)
    o_ref[0] = jnp.concatenate(outs, axis=-1).astype(o_ref.dtype)


def _na_attention(q, k, v, rpb):
    bsz, seq, _ = q.shape
    rows = seq // GRID_W
    n_steps = rows // NA_ROWS_PER_STEP
    nq = NA_ROWS_PER_STEP * GRID_W
    n_keys = NA_KEY_ROWS * GRID_W
    bias = _bias_lookup(rpb.reshape(NA_HEADS, -1), _na_index_tables(rows), "na_bias")

    def bias_map(b, i):
        return (jnp.where(i == 0, 0, jnp.where(i == n_steps - 1, 2, 1)), 0, 0, 0)

    return pl.pallas_call(
        functools.partial(_na_kernel, rows=rows),
        grid=(bsz, n_steps),
        in_specs=[pl.BlockSpec((1, nq, A_W), lambda b, i: (b, i, 0)),
                  pl.BlockSpec((1, seq, A_W), lambda b, i: (b, 0, 0)),
                  pl.BlockSpec((1, seq, A_W), lambda b, i: (b, 0, 0)),
                  pl.BlockSpec((1, NA_HEADS, nq, n_keys), bias_map)],
        out_specs=pl.BlockSpec((1, nq, A_W), lambda b, i: (b, i, 0)),
        out_shape=jax.ShapeDtypeStruct((bsz, seq, A_W), jnp.bfloat16),
        compiler_params=_params(("parallel", "arbitrary")),
        name="na_attention",
    )(q, k, v, bias)


def _sw_kernel(sink_ref, q_ref, k_ref, v_ref, bias_ref, o_ref, *, seq):
    i = pl.program_id(1)
    kstart = pl.multiple_of(jnp.clip(i * SW_TQ - SW_WINDOW, 0, seq - SW_TK), SW_WINDOW)
    k_win = k_ref[0, pl.ds(kstart, SW_TK), :]
    v_win = v_ref[0, pl.ds(kstart, SW_TK), :]
    q = q_ref[0]
    scale = HEAD_DIM ** -0.5
    group = SW_HEADS // SW_KV_HEADS
    outs = []
    for h in range(SW_HEADS):
        kv = h // group
        qsl = slice(h * HEAD_DIM, (h + 1) * HEAD_DIM)
        ksl = slice(kv * HEAD_DIM, (kv + 1) * HEAD_DIM)
        s = lax.dot_general(q[:, qsl], k_win[:, ksl], (((1,), (1,)), ((), ())),
                            preferred_element_type=jnp.float32)
        s = s * scale + bias_ref[0, h]
        sink = sink_ref[h]
        m = jnp.maximum(jnp.max(s, axis=-1, keepdims=True), sink)
        p = jnp.exp(s - m)
        l = jnp.sum(p, axis=-1, keepdims=True) + jnp.exp(sink - m)
        o = jnp.dot(p.astype(jnp.bfloat16), v_win[:, ksl], preferred_element_type=jnp.float32)
        outs.append(o / l)
    o_ref[0] = jnp.concatenate(outs, axis=-1).astype(o_ref.dtype)


def _sw_attention(q, k, v, sink, sw_table):
    bsz, seq, _ = q.shape
    n_steps = seq // SW_TQ
    bias = _bias_lookup(sw_table.T, _sw_index_tables(seq), "sw_bias")

    def bias_map(b, i):
        return (jnp.where(i == 0, 0, jnp.where(i == n_steps - 1, 2, 1)), 0, 0, 0)

    return pl.pallas_call(
        functools.partial(_sw_kernel, seq=seq),
        grid=(bsz, n_steps),
        in_specs=[pl.BlockSpec(memory_space=pltpu.SMEM),
                  pl.BlockSpec((1, SW_TQ, B_Q_W), lambda b, i: (b, i, 0)),
                  pl.BlockSpec((1, seq, B_KV_W), lambda b, i: (b, 0, 0)),
                  pl.BlockSpec((1, seq, B_KV_W), lambda b, i: (b, 0, 0)),
                  pl.BlockSpec((1, SW_HEADS, SW_TQ, SW_TK), bias_map)],
        out_specs=pl.BlockSpec((1, SW_TQ, B_Q_W), lambda b, i: (b, i, 0)),
        out_shape=jax.ShapeDtypeStruct((bsz, seq, B_Q_W), jnp.bfloat16),
        compiler_params=_params(("parallel", "arbitrary")),
        name="sw_attention",
    )(sink.astype(jnp.float32), q, k, v, bias)


def _df_kernel(q_t_ref, k_ref, v_t_ref, bias_ref, lq_ref, lk_ref, g_ref, o_ref,
               qm_scr, s_scr, m_scr, acc_scr, *, lam_init, n_chunks):
    i = pl.program_id(1)
    n_stat = 2 * DIFF_HEADS

    q_t = q_t_ref[0]
    zeros = jnp.zeros((C_QK_W, DF_TQ), jnp.bfloat16)
    for n in range(n_stat):
        rows = slice(n * DIFF_QK_DIM, (n + 1) * DIFF_QK_DIM)
        qm_scr[n] = zeros
        qm_scr[n, rows, :] = q_t[rows, :]
    m_scr[...] = jnp.full_like(m_scr, NEG)
    acc_scr[...] = jnp.zeros_like(acc_scr)
    ones = jnp.ones((DF_ONES_ROWS, DF_TK), jnp.bfloat16)

    def issue_scores(j, n):
        kstart = pl.multiple_of(j * DF_TK, DF_TK)
        tile = jnp.clip(j - i, -DF_NEAR - 1, DF_NEAR + 1) + DF_NEAR + 1
        s = jnp.dot(k_ref[0, pl.ds(kstart, DF_TK), :], qm_scr[n], preferred_element_type=jnp.float32)
        s_scr[n] = s + bias_ref[tile, n // 2]

    for n in range(DF_LOOKAHEAD):
        issue_scores(0, n)

    def chunk(j, carry):
        kstart = pl.multiple_of(j * DF_TK, DF_TK)
        j_next = jnp.minimum(j + 1, n_chunks - 1)
        for n in range(n_stat):
            ahead = n + DF_LOOKAHEAD
            if ahead < n_stat:
                issue_scores(j, ahead)
            else:
                issue_scores(j_next, ahead - n_stat)
            h = n // 2
            v_aug = jnp.concatenate(
                [v_t_ref[0, h * DIFF_V_DIM:(h + 1) * DIFF_V_DIM, pl.ds(kstart, DF_TK)], ones], axis=0)
            m_prev = m_scr[n]
            m_new = jnp.maximum(m_prev, jnp.max(s_scr[n], axis=0, keepdims=True))
            alpha = jnp.exp2(m_prev - m_new)
            p = jnp.exp2(s_scr[n] - m_new).astype(jnp.bfloat16)
            acc_scr[n] = alpha * acc_scr[n] + jnp.dot(v_aug, p, preferred_element_type=jnp.float32)
            m_scr[n] = m_new
        return carry

    lax.fori_loop(0, n_chunks, chunk, 0, unroll=2)

    dots = jnp.sum(lq_ref[...] * lk_ref[...], axis=-1, keepdims=True)
    lam = jnp.exp(dots[0:1]) - jnp.exp(dots[1:2]) + lam_init
    outs = []
    for h in range(DIFF_HEADS):
        a1 = acc_scr[2 * h]
        a2 = acc_scr[2 * h + 1]
        o1 = a1[:DIFF_V_DIM] / a1[DIFF_V_DIM:DIFF_V_DIM + 1]
        o2 = a2[:DIFF_V_DIM] / a2[DIFF_V_DIM:DIFF_V_DIM + 1]
        of = o1 - lam * o2
        of = of * lax.rsqrt(jnp.mean(of * of, axis=0, keepdims=True) + LN_EPS)
        outs.append(of * g_ref[...] * (1.0 - lam_init))
    o_ref[0] = jnp.concatenate(outs, axis=0).T.astype(o_ref.dtype)


def _df_attention(q_t, k, v_t, lam_q, lam_k, subln_g, diff_table, lam_init):
    bsz, seq, _ = k.shape
    n_chunks = seq // DF_TK
    assert DF_TQ == DF_TK and DF_LOOKAHEAD < 2 * DIFF_HEADS
    bias = _bias_lookup(diff_table.T * LOG2E, _df_bucket_tiles(), "diff_bias")
    n_stat = 2 * DIFF_HEADS
    n_tiles = 2 * DF_NEAR + 3
    const2 = lambda b, i: (0, 0)
    return pl.pallas_call(
        functools.partial(_df_kernel, lam_init=lam_init, n_chunks=n_chunks),
        grid=(bsz, seq // DF_TQ),
        in_specs=[pl.BlockSpec((1, C_QK_W, DF_TQ), lambda b, i: (b, 0, i)),
                  pl.BlockSpec((1, seq, C_QK_W), lambda b, i: (b, 0, 0)),
                  pl.BlockSpec((1, C_V_W, seq), lambda b, i: (b, 0, 0)),
                  pl.BlockSpec((n_tiles, DIFF_HEADS, DF_TK, DF_TQ), lambda b, i: (0, 0, 0, 0),
                               pipeline_mode=pl.Buffered(1)),
                  pl.BlockSpec((2, DIFF_QK_DIM), const2),
                  pl.BlockSpec((2, DIFF_QK_DIM), const2),
                  pl.BlockSpec((DIFF_V_DIM, 1), const2)],
        out_specs=pl.BlockSpec((1, DF_TQ, C_V_W), lambda b, i: (b, i, 0)),
        out_shape=jax.ShapeDtypeStruct((bsz, seq, C_V_W), jnp.bfloat16),
        scratch_shapes=[pltpu.VMEM((n_stat, C_QK_W, DF_TQ), jnp.bfloat16),
                        pltpu.VMEM((n_stat, DF_TK, DF_TQ), jnp.float32),
                        pltpu.VMEM((n_stat, 1, DF_TQ), jnp.float32),
                        pltpu.VMEM((n_stat, DF_ACC_ROWS, DF_TQ), jnp.float32)],
        compiler_params=_params(("parallel", "arbitrary")),
        name="diff_attention",
    )(q_t, k, v_t, bias, lam_q.astype(jnp.float32), lam_k.astype(jnp.float32),
      subln_g.astype(jnp.float32).reshape(DIFF_V_DIM, 1))


def _outproj_kernel(x_ref, oa_ref, ob_ref, oc_ref, w_ref, g_ref, b_ref, o_ref, *, alpha):
    mix = jnp.dot(oa_ref[...], w_ref[0:A_W, :], preferred_element_type=jnp.float32)
    mix += jnp.dot(ob_ref[...], w_ref[A_W:A_W + B_Q_W, :], preferred_element_type=jnp.float32)
    mix += jnp.dot(oc_ref[...], w_ref[A_W + B_Q_W:MIX_WIDTH, :], preferred_element_type=jnp.float32)
    o_ref[...] = _layer_norm_f32(alpha * x_ref[...] + mix, g_ref[...], b_ref[...])


def _outproj_ln(x2d, oa, ob, oc, w_bf16, g, b, alpha):
    n_tok = x2d.shape[0]
    row = lambda i: (i, 0)
    const = lambda i: (0, 0)
    return pl.pallas_call(
        functools.partial(_outproj_kernel, alpha=alpha),
        grid=(n_tok // ROW_TILE,),
        in_specs=[pl.BlockSpec((ROW_TILE, D_MODEL), row),
                  pl.BlockSpec((ROW_TILE, A_W), row),
                  pl.BlockSpec((ROW_TILE, B_Q_W), row),
                  pl.BlockSpec((ROW_TILE, C_V_W), row),
                  pl.BlockSpec((MIX_WIDTH, D_MODEL), const),
                  pl.BlockSpec((1, D_MODEL), const),
                  pl.BlockSpec((1, D_MODEL), const)],
        out_specs=pl.BlockSpec((ROW_TILE, D_MODEL), row),
        out_shape=jax.ShapeDtypeStruct((n_tok, D_MODEL), jnp.float32),
        compiler_params=_params(("parallel",)),
        name="outproj_ln",
    )(x2d, oa, ob, oc, w_bf16, g.reshape(1, D_MODEL), b.reshape(1, D_MODEL))


FF_CHUNK = 1024


def _ffn_kernel(x_ref, w1_ref, w2_ref, g_ref, b_ref, o_ref, *, alpha):
    x = x_ref[...]
    xb = x.astype(jnp.bfloat16)
    y = jnp.zeros((ROW_TILE, D_MODEL), jnp.float32)
    for c in range(D_FF // FF_CHUNK):
        sl = slice(c * FF_CHUNK, (c + 1) * FF_CHUNK)
        h = jnp.maximum(jnp.dot(xb, w1_ref[:, sl], preferred_element_type=jnp.float32), 0.0)
        y += jnp.dot((h * h).astype(jnp.bfloat16), w2_ref[sl, :], preferred_element_type=jnp.float32)
    o_ref[...] = _layer_norm_f32(alpha * x + y, g_ref[...], b_ref[...])


def _ffn_ln(x2d, w1_bf16, w2_bf16, g, b, alpha):
    n_tok = x2d.shape[0]
    row = lambda i: (i, 0)
    const = lambda i: (0, 0)
    return pl.pallas_call(
        functools.partial(_ffn_kernel, alpha=alpha),
        grid=(n_tok // ROW_TILE,),
        in_specs=[pl.BlockSpec((ROW_TILE, D_MODEL), row),
                  pl.BlockSpec((D_MODEL, D_FF), const, pipeline_mode=pl.Buffered(1)),
                  pl.BlockSpec((D_FF, D_MODEL), const, pipeline_mode=pl.Buffered(1)),
                  pl.BlockSpec((1, D_MODEL), const),
                  pl.BlockSpec((1, D_MODEL), const)],
        out_specs=pl.BlockSpec((ROW_TILE, D_MODEL), row),
        out_shape=jax.ShapeDtypeStruct((n_tok, D_MODEL), jnp.float32),
        compiler_params=_params(("parallel",)),
        name="ffn_ln",
    )(x2d, w1_bf16, w2_bf16, g.reshape(1, D_MODEL), b.reshape(1, D_MODEL))


def kernel(x, ln_in_g, ln_in_b, t5_table, w_in, w_out, na_rpb, sw_sink, diff_lam_q, diff_lam_k, diff_subln_g,
           ln_mix_g, ln_mix_b, w_ff1, w_ff2, ln_ff_g, ln_ff_b):
    alpha = (2 * DEPTH) ** 0.25
    bsz, seq, _ = x.shape
    assert x.shape[-1] == D_MODEL and seq % max(ROW_TILE, SW_TQ, DF_TQ, DF_TK) == 0
    assert (seq // GRID_W) % NA_ROWS_PER_STEP == 0 and seq // GRID_W >= NA_KEY_ROWS
    sw_table = t5_table[:, :SW_HEADS]
    diff_table = t5_table[:, SW_HEADS:]
    w_in_b = w_in.astype(jnp.bfloat16)
    w_out_b = w_out.astype(jnp.bfloat16)
    w_ff1_b = w_ff1.astype(jnp.bfloat16)
    w_ff2_b = w_ff2.astype(jnp.bfloat16)
    x2d = x.reshape(bsz * seq, D_MODEL)
    for l in range(DEPTH):
        lam_init = 0.8 - 0.6 * math.exp(-0.3 * l)
        if l == 0:
            x2d, *proj = _inproj(x2d, ln_in_g, ln_in_b, w_in_b[l], seq, True)
        else:
            proj = _inproj(x2d, ln_in_g, ln_in_b, w_in_b[l], seq, False)
        *tok_major, qc_t, vc_t = proj
        qa, ka, va, qb, kb, vb, kc = [p.reshape(bsz, seq, -1) for p in tok_major]
        oa = _na_attention(qa, ka, va, na_rpb[l])
        ob = _sw_attention(qb, kb, vb, sw_sink[l], sw_table)
        oc = _df_attention(qc_t, kc, vc_t, diff_lam_q[l], diff_lam_k[l], diff_subln_g[l], diff_table, lam_init)
        x2d = _outproj_ln(x2d, oa.reshape(-1, A_W), ob.reshape(-1, B_Q_W), oc.reshape(-1, C_V_W),
                          w_out_b[l], ln_mix_g[l], ln_mix_b[l], alpha)
        x2d = _ffn_ln(x2d, w_ff1_b[l], w_ff2_b[l], ln_ff_g[l], ln_ff_b[l], alpha)
    return x2d.reshape(bsz, seq, D_MODEL)
```

```python
import functools
import math

import numpy as np
import jax
import jax.numpy as jnp
from jax import lax
from jax.experimental import pallas as pl
from jax.experimental.pallas import tpu as pltpu

D_MODEL = 1024
DEPTH = 2
GRID_W = 64
HEAD_DIM = 64
NA_HEADS = 4
NA_WIN_ROWS = 8
NA_WIN_COLS = 16
SW_HEADS = 8
SW_KV_HEADS = 2
SW_WINDOW = 128
DIFF_HEADS = 4
DIFF_QK_DIM = HEAD_DIM // 2
DIFF_V_DIM = HEAD_DIM
D_FF = 4 * D_MODEL
N_BUCKETS = 32
MAX_DISTANCE = 128
LN_EPS = 1e-5
NEG = -1e30
LOG2E = math.log2(math.e)

A_W = NA_HEADS * HEAD_DIM
B_Q_W = SW_HEADS * HEAD_DIM
B_KV_W = SW_KV_HEADS * HEAD_DIM
C_QK_W = DIFF_HEADS * 2 * DIFF_QK_DIM
C_V_W = DIFF_HEADS * DIFF_V_DIM
IN_SPLITS = (A_W, A_W, A_W, B_Q_W, B_KV_W, B_KV_W, C_QK_W, C_QK_W, C_V_W)
IN_OFFSETS = tuple(int(v) for v in np.cumsum((0,) + IN_SPLITS[:-1]))
IN_WIDTH = sum(IN_SPLITS)
MIX_WIDTH = A_W + B_Q_W + C_V_W
TOKEN_MAJOR_SPLITS = (0, 1, 2, 3, 4, 5, 7)
QC_SPLIT, VC_SPLIT = 6, 8

VMEM_LIMIT_BYTES = 56 * 1024 * 1024

ROW_TILE = 512
NA_ROWS_PER_STEP = 4
SW_TQ = 256
SW_TK = SW_TQ + 2 * SW_WINDOW
DF_TQ = 256
DF_TK = 256
DF_ONES_ROWS = 16
DF_ACC_ROWS = DIFF_V_DIM + DF_ONES_ROWS
DF_NEAR = MAX_DISTANCE // DF_TK + 1
DF_LOOKAHEAD = 5
DF_QM_ROWS = 128
BIAS_STRIP_ELEMS = 8 * 1024


def _params(semantics):
    return pltpu.CompilerParams(dimension_semantics=semantics, vmem_limit_bytes=VMEM_LIMIT_BYTES)


def _layer_norm_f32(x, g, b):
    mu = jnp.mean(x, axis=-1, keepdims=True)
    xc = x - mu
    var = jnp.mean(xc * xc, axis=-1, keepdims=True)
    return xc * lax.rsqrt(var + LN_EPS) * g + b


def _t5_bucket_np(rel):
    rel = np.asarray(rel, dtype=np.int64)
    nb = N_BUCKETS // 2
    max_exact = nb // 2
    n = np.abs(rel)
    nn = np.maximum(n, 1)
    floor_log2_sq = np.floor(np.log2((nn * nn).astype(np.float64)) + 1e-9).astype(np.int64)
    large = np.minimum(max_exact + floor_log2_sq - 6, nb - 1)
    return (np.where(rel > 0, nb, 0) + np.where(n < max_exact, n, large)).astype(np.int32)


def _na_index_tables():
    n_dr = 2 * NA_WIN_ROWS - 1
    n_dc = 2 * NA_WIN_COLS - 1
    p = np.arange(n_dr - 1)[:, None, None]
    c = np.arange(GRID_W)[None, :, None]
    lane = np.arange(2 * GRID_W)[None, None, :]
    kc = lane % GRID_W
    col_start = np.clip(c - NA_WIN_COLS // 2, 0, GRID_W - NA_WIN_COLS)
    valid = (kc >= col_start) & (kc < col_start + NA_WIN_COLS)
    dc = np.clip(kc - c, -(NA_WIN_COLS - 1), NA_WIN_COLS - 1) + NA_WIN_COLS - 1
    idx = (p + lane // GRID_W) * n_dc + dc
    return np.where(valid, idx, -1).astype(np.int32)


def _sw_index_tables(seq):
    n_steps = seq // SW_TQ
    buckets = []
    for step in (0, 1, n_steps - 1):
        qbase = step * SW_TQ
        kstart = int(np.clip(qbase - SW_WINDOW, 0, seq - SW_TK))
        rel = (kstart + np.arange(SW_TK)[None, :]) - (qbase + np.arange(SW_TQ)[:, None])
        buckets.append(np.where(np.abs(rel) <= SW_WINDOW, _t5_bucket_np(rel), -1))
    return np.stack(buckets).astype(np.int32)


def _df_bucket_tiles():
    tiles = []
    for d in range(-DF_NEAR - 1, DF_NEAR + 2):
        rel = d * DF_TK + np.arange(DF_TK)[:, None] - np.arange(DF_TQ)[None, :]
        tiles.append(_t5_bucket_np(rel))
    tiles = np.stack(tiles)
    assert (tiles[0] == tiles[0, 0, 0]).all() and (tiles[-1] == tiles[-1, 0, 0]).all()
    return tiles


def _bias_lookup_kernel(table_ref, idx_ref, o_ref, *, n_entries):
    h = pl.program_id(1)
    _, rows, cols = idx_ref.shape
    strip_rows = min(rows, max(8, BIAS_STRIP_ELEMS // cols))
    assert rows % strip_rows == 0

    def strip(r, carry):
        r0 = pl.multiple_of(r * strip_rows, strip_rows)
        idx = idx_ref[0, pl.ds(r0, strip_rows), :]
        acc = jnp.full(idx.shape, NEG, jnp.float32)
        for e in range(n_entries):
            acc = jnp.where(idx == e, table_ref[h, e], acc)
        o_ref[0, 0, pl.ds(r0, strip_rows), :] = acc
        return carry

    lax.fori_loop(0, rows // strip_rows, strip, 0)


def _bias_lookup(table, idx, name):
    n_heads, n_entries = table.shape
    n_tiles, rows, cols = idx.shape
    return pl.pallas_call(
        functools.partial(_bias_lookup_kernel, n_entries=n_entries),
        grid=(n_tiles, n_heads),
        in_specs=[pl.BlockSpec(memory_space=pltpu.SMEM),
                  pl.BlockSpec((1, rows, cols), lambda t, h: (t, 0, 0))],
        out_specs=pl.BlockSpec((1, 1, rows, cols), lambda t, h: (t, h, 0, 0)),
        out_shape=jax.ShapeDtypeStruct((n_tiles, n_heads, rows, cols), jnp.float32),
        compiler_params=_params(("arbitrary", "arbitrary")),
        name=name,
    )(table.astype(jnp.float32), jnp.asarray(idx))


def _inproj_kernel(x_ref, g_ref, b_ref, w_ref, wq_t_ref, wv_t_ref, *out_refs, apply_ln):
    x = x_ref[...]
    if apply_ln:
        xn_ref, *out_refs = out_refs
        x = _layer_norm_f32(x, g_ref[...], b_ref[...])
        xn_ref[...] = x
    *proj_refs, qc_t_ref, vc_t_ref = out_refs
    xb = x.astype(jnp.bfloat16)
    for split, o_ref in zip(TOKEN_MAJOR_SPLITS, proj_refs):
        off, width = IN_OFFSETS[split], IN_SPLITS[split]
        o_ref[...] = jnp.dot(xb, w_ref[:, off:off + width],
                             preferred_element_type=jnp.float32).astype(o_ref.dtype)
    nt_dims = (((1,), (1,)), ((), ()))
    qc_t = lax.dot_general(wq_t_ref[...], xb, nt_dims, preferred_element_type=jnp.float32)
    qc_t_ref[0] = (qc_t * (DIFF_QK_DIM ** -0.5 * LOG2E)).astype(qc_t_ref.dtype)
    vc_t = lax.dot_general(wv_t_ref[...], xb, nt_dims, preferred_element_type=jnp.float32)
    vc_t_ref[0] = vc_t.astype(vc_t_ref.dtype)


def _inproj(x2d, g, b, w_bf16, seq, apply_ln):
    n_tok = x2d.shape[0]
    bsz = n_tok // seq
    tiles_per_seq = seq // ROW_TILE
    row = lambda i: (i, 0)
    const = lambda i: (0, 0)
    feat_major = lambda i: (i // tiles_per_seq, 0, i % tiles_per_seq)
    out_shape = [jax.ShapeDtypeStruct((n_tok, IN_SPLITS[s]), jnp.bfloat16) for s in TOKEN_MAJOR_SPLITS]
    out_specs = [pl.BlockSpec((ROW_TILE, IN_SPLITS[s]), row) for s in TOKEN_MAJOR_SPLITS]
    for width in (C_QK_W, C_V_W):
        out_shape.append(jax.ShapeDtypeStruct((bsz, width, seq), jnp.bfloat16))
        out_specs.append(pl.BlockSpec((1, width, ROW_TILE), feat_major))
    if apply_ln:
        out_shape = [jax.ShapeDtypeStruct((n_tok, D_MODEL), jnp.float32)] + out_shape
        out_specs = [pl.BlockSpec((ROW_TILE, D_MODEL), row)] + out_specs
    wq_t = w_bf16[:, IN_OFFSETS[QC_SPLIT]:IN_OFFSETS[QC_SPLIT] + C_QK_W].T
    wv_t = w_bf16[:, IN_OFFSETS[VC_SPLIT]:IN_OFFSETS[VC_SPLIT] + C_V_W].T
    return pl.pallas_call(
        functools.partial(_inproj_kernel, apply_ln=apply_ln),
        grid=(n_tok // ROW_TILE,),
        in_specs=[pl.BlockSpec((ROW_TILE, D_MODEL), row),
                  pl.BlockSpec((1, D_MODEL), const),
                  pl.BlockSpec((1, D_MODEL), const),
                  pl.BlockSpec((D_MODEL, IN_WIDTH), const),
                  pl.BlockSpec((C_QK_W, D_MODEL), const),
                  pl.BlockSpec((C_V_W, D_MODEL), const)],
        out_specs=out_specs,
        out_shape=out_shape,
        compiler_params=_params(("parallel",)),
        name="inproj_ln" if apply_ln else "inproj",
    )(x2d, g.reshape(1, D_MODEL), b.reshape(1, D_MODEL), w_bf16, wq_t, wv_t)


def _na_kernel(q_ref, k_ref, v_ref, t2_ref, o_ref, *, rows):
    i = pl.program_id(1)
    r0 = i * NA_ROWS_PER_STEP
    pair_w = 2 * HEAD_DIM
    win_keys = NA_WIN_ROWS * GRID_W
    low = lax.broadcasted_iota(jnp.int32, (GRID_W, pair_w), 1) < HEAD_DIM
    nt_dims = (((1,), (1,)), ((), ()))
    stages = [(ri, hp) for ri in range(NA_ROWS_PER_STEP) for hp in range(NA_HEADS // 2)]

    def scores(ri, hp):
        r = r0 + ri
        row_start = jnp.clip(r - NA_WIN_ROWS // 2, 0, rows - NA_WIN_ROWS)
        kstart = pl.multiple_of(row_start * GRID_W, GRID_W)
        lanes = slice(hp * pair_w, (hp + 1) * pair_w)
        q_pair = q_ref[0, ri * GRID_W:(ri + 1) * GRID_W, lanes].astype(jnp.float32)
        q_both = jnp.concatenate([jnp.where(low, q_pair, 0.0), jnp.where(low, 0.0, q_pair)], axis=0)
        k_pair = k_ref[0, pl.ds(kstart, win_keys), lanes]
        s = lax.dot_general(q_both.astype(jnp.bfloat16), k_pair, nt_dims,
                            preferred_element_type=jnp.float32)
        p0 = row_start - r + NA_WIN_ROWS - 1
        bias = jnp.concatenate(
            [jnp.concatenate([t2_ref[p0 + 2 * kk, 2 * hp + e] for kk in range(NA_WIN_ROWS // 2)], axis=1)
             for e in range(2)], axis=0)
        return s * (HEAD_DIM ** -0.5 * LOG2E) + bias, kstart

    def finish(ri, hp, s, kstart):
        lanes = slice(hp * pair_w, (hp + 1) * pair_w)
        m = jnp.max(s, axis=-1, keepdims=True)
        p = jnp.exp2(s - m)
        l = jnp.sum(p, axis=-1, keepdims=True)
        v_pair = v_ref[0, pl.ds(kstart, win_keys), lanes]
        o = jnp.dot(p.astype(jnp.bfloat16), v_pair, preferred_element_type=jnp.float32) / l
        o_ref[0, ri * GRID_W:(ri + 1) * GRID_W, lanes] = jnp.where(low, o[:GRID_W], o[GRID_W:]).astype(o_ref.dtype)

    pending = scores(*stages[0])
    for t, (ri, hp) in enumerate(stages):
        current = pending
        if t + 1 < len(stages):
            pending = scores(*stages[t + 1])
        finish(ri, hp, *current)


def _na_attention(q, k, v, rpb):
    bsz, seq, _ = q.shape
    rows = seq // GRID_W
    n_steps = rows // NA_ROWS_PER_STEP
    nq = NA_ROWS_PER_STEP * GRID_W
    idx = _na_index_tables()
    t2 = _bias_lookup(rpb.reshape(NA_HEADS, -1).astype(jnp.float32) * LOG2E, idx, "na_bias")
    return pl.pallas_call(
        functools.partial(_na_kernel, rows=rows),
        grid=(bsz, n_steps),
        in_specs=[pl.BlockSpec((1, nq, A_W), lambda b, i: (b, i, 0)),
                  pl.BlockSpec((1, seq, A_W), lambda b, i: (b, 0, 0)),
                  pl.BlockSpec((1, seq, A_W), lambda b, i: (b, 0, 0)),
                  pl.BlockSpec(t2.shape, lambda b, i: (0, 0, 0, 0))],
        out_specs=pl.BlockSpec((1, nq, A_W), lambda b, i: (b, i, 0)),
        out_shape=jax.ShapeDtypeStruct((bsz, seq, A_W), jnp.bfloat16),
        compiler_params=_params(("parallel", "arbitrary")),
        name="na_attention",
    )(q, k, v, t2)


def _sw_head_order():
    group = SW_HEADS // SW_KV_HEADS
    assert SW_KV_HEADS == 2
    return [h for g in range(group) for h in (g, g + group)]


def _permute_head_blocks(w, axis, start, order):
    def block(lo, hi):
        return lax.slice_in_dim(w, lo, hi, axis=axis)
    end = start + len(order) * HEAD_DIM
    parts = [block(0, start)] + [block(start + h * HEAD_DIM, start + (h + 1) * HEAD_DIM) for h in order]
    parts.append(block(end, w.shape[axis]))
    return jnp.concatenate([p for p in parts if p.shape[axis] > 0], axis=axis)


def _sw_kernel(sink_ref, q_ref, k_ref, v_ref, bias_ref, o_ref, *, seq):
    i = pl.program_id(1)
    kstart = pl.multiple_of(jnp.clip(i * SW_TQ - SW_WINDOW, 0, seq - SW_TK), SW_WINDOW)
    k_win = k_ref[0, pl.ds(kstart, SW_TK), :]
    v_win = v_ref[0, pl.ds(kstart, SW_TK), :]
    group = SW_HEADS // SW_KV_HEADS
    pair_w = 2 * HEAD_DIM
    low = lax.broadcasted_iota(jnp.int32, (SW_TQ, pair_w), 1) < HEAD_DIM
    first = lax.broadcasted_iota(jnp.int32, (2 * SW_TQ, 1), 0) < SW_TQ
    nt_dims = (((1,), (1,)), ((), ()))

    def scores(g):
        q_grp = q_ref[0, :, g * pair_w:(g + 1) * pair_w].astype(jnp.float32)
        q_both = jnp.concatenate([jnp.where(low, q_grp, 0.0), jnp.where(low, 0.0, q_grp)], axis=0)
        s = lax.dot_general(q_both.astype(jnp.bfloat16), k_win, nt_dims,
                            preferred_element_type=jnp.float32)
        bias = jnp.concatenate([bias_ref[0, g], bias_ref[0, g + group]], axis=0)
        return s * (HEAD_DIM ** -0.5 * LOG2E) + bias

    def finish(g, s):
        sink = jnp.where(first, sink_ref[g] * LOG2E, sink_ref[g + group] * LOG2E)
        m = jnp.maximum(jnp.max(s, axis=-1, keepdims=True), sink)
        p = jnp.exp2(s - m)
        l = jnp.sum(p, axis=-1, keepdims=True) + jnp.exp2(sink - m)
        o = jnp.dot(p.astype(jnp.bfloat16), v_win, preferred_element_type=jnp.float32) / l
        o_ref[0, :, g * pair_w:(g + 1) * pair_w] = jnp.where(low, o[:SW_TQ], o[SW_TQ:]).astype(o_ref.dtype)

    pending = scores(0)
    for g in range(group):
        current = pending
        if g + 1 < group:
            pending = scores(g + 1)
        finish(g, current)


def _sw_attention(q, k, v, sink, sw_table):
    bsz, seq, _ = q.shape
    n_steps = seq // SW_TQ
    bias = _bias_lookup(sw_table.T.astype(jnp.float32) * LOG2E, _sw_index_tables(seq), "sw_bias")

    def bias_map(b, i):
        return (jnp.where(i == 0, 0, jnp.where(i == n_steps - 1, 2, 1)), 0, 0, 0)

    return pl.pallas_call(
        functools.partial(_sw_kernel, seq=seq),
        grid=(bsz, n_steps),
        in_specs=[pl.BlockSpec(memory_space=pltpu.SMEM),
                  pl.BlockSpec((1, SW_TQ, B_Q_W), lambda b, i: (b, i, 0)),
                  pl.BlockSpec((1, seq, B_KV_W), lambda b, i: (b, 0, 0)),
                  pl.BlockSpec((1, seq, B_KV_W), lambda b, i: (b, 0, 0)),
                  pl.BlockSpec((1, SW_HEADS, SW_TQ, SW_TK), bias_map)],
        out_specs=pl.BlockSpec((1, SW_TQ, B_Q_W), lambda b, i: (b, i, 0)),
        out_shape=jax.ShapeDtypeStruct((bsz, seq, B_Q_W), jnp.bfloat16),
        compiler_params=_params(("parallel", "arbitrary")),
        name="sw_attention",
    )(sink.astype(jnp.float32), q, k, v, bias)


def _df_kernel(cfar_ref, q_t_ref, k_ref, v_t_ref, bias_ref, lq_ref, lk_ref, g_ref, o_ref,
               qm_scr, s_scr, mx_scr, m_scr, acc_scr, *, lam_init, n_chunks):
    i = pl.program_id(1)
    n_stat = 2 * DIFF_HEADS
    n_near = 2 * DF_NEAR + 2
    n_far = n_chunks - n_near
    per_group = DF_QM_ROWS // DIFF_QK_DIM

    q_t = q_t_ref[0]
    zeros = jnp.zeros((DF_QM_ROWS, DF_TQ), jnp.bfloat16)
    for n in range(n_stat):
        qm_scr[n] = zeros
        dst = (n % per_group) * DIFF_QK_DIM
        qm_scr[n, dst:dst + DIFF_QK_DIM, :] = q_t[n * DIFF_QK_DIM:(n + 1) * DIFF_QK_DIM, :]
    m_scr[...] = jnp.full_like(m_scr, NEG)
    acc_scr[...] = jnp.zeros_like(acc_scr)
    ones = jnp.ones((DF_ONES_ROWS, DF_TK), jnp.bfloat16)

    near0 = jnp.clip(i - DF_NEAR, 0, n_far)

    def near_chunk(u):
        return near0 + u

    def far_chunk(t):
        return t + jnp.where(t >= near0, n_near, 0)

    def issue_scores(j, n, near):
        kstart = pl.multiple_of(j * DF_TK, DF_TK)
        g = n // per_group
        k_grp = k_ref[0, pl.ds(kstart, DF_TK), g * DF_QM_ROWS:(g + 1) * DF_QM_ROWS]
        s = jnp.dot(k_grp, qm_scr[n], preferred_element_type=jnp.float32)
        if near:
            tile = jnp.clip(j - i, -DF_NEAR - 1, DF_NEAR + 1) + DF_NEAR + 1
            s = s + bias_ref[tile, n // 2]
        s_scr[n] = s
        mx_scr[n] = jnp.max(s, axis=0, keepdims=True)

    def consume(j, n, cb):
        kstart = pl.multiple_of(j * DF_TK, DF_TK)
        h = n // 2
        v_aug = jnp.concatenate(
            [v_t_ref[0, h * DIFF_V_DIM:(h + 1) * DIFF_V_DIM, pl.ds(kstart, DF_TK)], ones], axis=0)
        m_prev = m_scr[n]
        m_new = jnp.maximum(m_prev, mx_scr[n] + cb)
        alpha = jnp.exp2(m_prev - m_new)
        p = jnp.exp2(s_scr[n] - (m_new - cb)).astype(jnp.bfloat16)
        acc_scr[n] = alpha * acc_scr[n] + jnp.dot(v_aug, p, preferred_element_type=jnp.float32)
        m_scr[n] = m_new

    def run_chunk(j, j_next, near, next_near, cbs):
        for n in range(n_stat):
            ahead = n + DF_LOOKAHEAD
            if ahead < n_stat:
                issue_scores(j, ahead, near)
            else:
                issue_scores(j_next, ahead - n_stat, next_near)
            consume(j, n, cbs[n // 2])

    for n in range(DF_LOOKAHEAD):
        issue_scores(near_chunk(0), n, True)
    for u in range(n_near):
        last = u == n_near - 1
        run_chunk(near_chunk(u), far_chunk(0) if last else near_chunk(u + 1), True, not last,
                  [0.0] * DIFF_HEADS)

    def far_pair(it, carry):
        for t in (2 * it, 2 * it + 1):
            j = far_chunk(t)
            j_next = far_chunk(jnp.minimum(t + 1, n_far - 1))
            side = (j > i).astype(jnp.int32)
            run_chunk(j, j_next, False, False, [cfar_ref[side, h] for h in range(DIFF_HEADS)])
        return carry

    lax.fori_loop(0, n_far // 2, far_pair, 0)

    dots = jnp.sum(lq_ref[...] * lk_ref[...], axis=-1, keepdims=True)
    lam = jnp.exp(dots[0:1]) - jnp.exp(dots[1:2]) + lam_init
    outs = []
    for h in range(DIFF_HEADS):
        a1 = acc_scr[2 * h]
        a2 = acc_scr[2 * h + 1]
        o1 = a1[:DIFF_V_DIM] / a1[DIFF_V_DIM:DIFF_V_DIM + 1]
        o2 = a2[:DIFF_V_DIM] / a2[DIFF_V_DIM:DIFF_V_DIM + 1]
        of = o1 - lam * o2
        of = of * lax.rsqrt(jnp.mean(of * of, axis=0, keepdims=True) + LN_EPS)
        outs.append(of * g_ref[...] * (1.0 - lam_init))
    o_ref[0] = jnp.concatenate(outs, axis=0).T.astype(o_ref.dtype)


def _df_attention(q_t, k, v_t, lam_q, lam_k, subln_g, diff_table, lam_init):
    bsz, seq, _ = k.shape
    n_chunks = seq // DF_TK
    assert DF_TQ == DF_TK and DF_LOOKAHEAD < 2 * DIFF_HEADS
    assert (n_chunks - 2 * DF_NEAR - 2) % 2 == 0 and n_chunks > 2 * DF_NEAR + 2
    table_t = diff_table.astype(jnp.float32).T * LOG2E
    bias = _bias_lookup(table_t, _df_bucket_tiles(), "diff_bias")
    cfar = jnp.stack([table_t[:, N_BUCKETS // 2 - 1], table_t[:, N_BUCKETS - 1]])
    n_stat = 2 * DIFF_HEADS
    n_tiles = 2 * DF_NEAR + 3
    const2 = lambda b, i: (0, 0)
    return pl.pallas_call(
        functools.partial(_df_kernel, lam_init=lam_init, n_chunks=n_chunks),
        grid=(bsz, seq // DF_TQ),
        in_specs=[pl.BlockSpec(memory_space=pltpu.SMEM),
                  pl.BlockSpec((1, C_QK_W, DF_TQ), lambda b, i: (b, 0, i)),
                  pl.BlockSpec((1, seq, C_QK_W), lambda b, i: (b, 0, 0)),
                  pl.BlockSpec((1, C_V_W, seq), lambda b, i: (b, 0, 0)),
                  pl.BlockSpec((n_tiles, DIFF_HEADS, DF_TK, DF_TQ), lambda b, i: (0, 0, 0, 0),
                               pipeline_mode=pl.Buffered(1)),
                  pl.BlockSpec((2, DIFF_QK_DIM), const2),
                  pl.BlockSpec((2, DIFF_QK_DIM), const2),
                  pl.BlockSpec((DIFF_V_DIM, 1), const2)],
        out_specs=pl.BlockSpec((1, DF_TQ, C_V_W), lambda b, i: (b, i, 0)),
        out_shape=jax.ShapeDtypeStruct((bsz, seq, C_V_W), jnp.bfloat16),
        scratch_shapes=[pltpu.VMEM((n_stat, DF_QM_ROWS, DF_TQ), jnp.bfloat16),
                        pltpu.VMEM((n_stat, DF_TK, DF_TQ), jnp.float32),
                        pltpu.VMEM((n_stat, 1, DF_TQ), jnp.float32),
                        pltpu.VMEM((n_stat, 1, DF_TQ), jnp.float32),
                        pltpu.VMEM((n_stat, DF_ACC_ROWS, DF_TQ), jnp.float32)],
        compiler_params=_params(("parallel", "arbitrary")),
        name="diff_attention",
    )(cfar, q_t, k, v_t, bias, lam_q.astype(jnp.float32), lam_k.astype(jnp.float32),
      subln_g.astype(jnp.float32).reshape(DIFF_V_DIM, 1))


def _outproj_kernel(x_ref, oa_ref, ob_ref, oc_ref, w_ref, g_ref, b_ref, o_ref, *, alpha):
    mix = jnp.dot(oa_ref[...], w_ref[0:A_W, :], preferred_element_type=jnp.float32)
    mix += jnp.dot(ob_ref[...], w_ref[A_W:A_W + B_Q_W, :], preferred_element_type=jnp.float32)
    mix += jnp.dot(oc_ref[...], w_ref[A_W + B_Q_W:MIX_WIDTH, :], preferred_element_type=jnp.float32)
    o_ref[...] = _layer_norm_f32(alpha * x_ref[...] + mix, g_ref[...], b_ref[...])


def _outproj_ln(x2d, oa, ob, oc, w_bf16, g, b, alpha):
    n_tok = x2d.shape[0]
    row = lambda i: (i, 0)
    const = lambda i: (0, 0)
    return pl.pallas_call(
        functools.partial(_outproj_kernel, alpha=alpha),
        grid=(n_tok // ROW_TILE,),
        in_specs=[pl.BlockSpec((ROW_TILE, D_MODEL), row),
                  pl.BlockSpec((ROW_TILE, A_W), row),
                  pl.BlockSpec((ROW_TILE, B_Q_W), row),
                  pl.BlockSpec((ROW_TILE, C_V_W), row),
                  pl.BlockSpec((MIX_WIDTH, D_MODEL), const),
                  pl.BlockSpec((1, D_MODEL), const),
                  pl.BlockSpec((1, D_MODEL), const)],
        out_specs=pl.BlockSpec((ROW_TILE, D_MODEL), row),
        out_shape=jax.ShapeDtypeStruct((n_tok, D_MODEL), jnp.float32),
        compiler_params=_params(("parallel",)),
        name="outproj_ln",
    )(x2d, oa, ob, oc, w_bf16, g.reshape(1, D_MODEL), b.reshape(1, D_MODEL))


FF_CHUNK = 1024


def _ffn_kernel(x_ref, w1_ref, w2_ref, g_ref, b_ref, o_ref, *, alpha):
    x = x_ref[...]
    xb = x.astype(jnp.bfloat16)
    y = jnp.zeros((ROW_TILE, D_MODEL), jnp.float32)
    for c in range(D_FF // FF_CHUNK):
        sl = slice(c * FF_CHUNK, (c + 1) * FF_CHUNK)
        h = jnp.maximum(jnp.dot(xb, w1_ref[:, sl], preferred_element_type=jnp.float32), 0.0)
        y += jnp.dot((h * h).astype(jnp.bfloat16), w2_ref[sl, :], preferred_element_type=jnp.float32)
    o_ref[...] = _layer_norm_f32(alpha * x + y, g_ref[...], b_ref[...])


def _ffn_ln(x2d, w1_bf16, w2_bf16, g, b, alpha):
    n_tok = x2d.shape[0]
    row = lambda i: (i, 0)
    const = lambda i: (0, 0)
    return pl.pallas_call(
        functools.partial(_ffn_kernel, alpha=alpha),
        grid=(n_tok // ROW_TILE,),
        in_specs=[pl.BlockSpec((ROW_TILE, D_MODEL), row),
                  pl.BlockSpec((D_MODEL, D_FF), const, pipeline_mode=pl.Buffered(1)),
                  pl.BlockSpec((D_FF, D_MODEL), const, pipeline_mode=pl.Buffered(1)),
                  pl.BlockSpec((1, D_MODEL), const),
                  pl.BlockSpec((1, D_MODEL), const)],
        out_specs=pl.BlockSpec((ROW_TILE, D_MODEL), row),
        out_shape=jax.ShapeDtypeStruct((n_tok, D_MODEL), jnp.float32),
        compiler_params=_params(("parallel",)),
        name="ffn_ln",
    )(x2d, w1_bf16, w2_bf16, g.reshape(1, D_MODEL), b.reshape(1, D_MODEL))


def kernel(x, ln_in_g, ln_in_b, t5_table, w_in, w_out, na_rpb, sw_sink, diff_lam_q, diff_lam_k, diff_subln_g,
           ln_mix_g, ln_mix_b, w_ff1, w_ff2, ln_ff_g, ln_ff_b):
    alpha = (2 * DEPTH) ** 0.25
    bsz, seq, _ = x.shape
    assert x.shape[-1] == D_MODEL and seq % max(ROW_TILE, SW_TQ, DF_TQ, DF_TK) == 0
    assert (seq // GRID_W) % NA_ROWS_PER_STEP == 0 and seq // GRID_W >= NA_WIN_ROWS
    sw_table = t5_table[:, :SW_HEADS]
    diff_table = t5_table[:, SW_HEADS:]
    w_in_b = _permute_head_blocks(w_in.astype(jnp.bfloat16), 2, IN_OFFSETS[3], _sw_head_order())
    w_out_b = _permute_head_blocks(w_out.astype(jnp.bfloat16), 1, A_W, _sw_head_order())
    w_ff1_b = w_ff1.astype(jnp.bfloat16)
    w_ff2_b = w_ff2.astype(jnp.bfloat16)
    x2d = x.reshape(bsz * seq, D_MODEL)
    for l in range(DEPTH):
        lam_init = 0.8 - 0.6 * math.exp(-0.3 * l)
        if l == 0:
            x2d, *proj = _inproj(x2d, ln_in_g, ln_in_b, w_in_b[l], seq, True)
        else:
            proj = _inproj(x2d, ln_in_g, ln_in_b, w_in_b[l], seq, False)
        *tok_major, qc_t, vc_t = proj
        qa, ka, va, qb, kb, vb, kc = [p.reshape(bsz, seq, -1) for p in tok_major]
        oa = _na_attention(qa, ka, va, na_rpb[l])
        ob = _sw_attention(qb, kb, vb, sw_sink[l], sw_table)
        oc = _df_attention(qc_t, kc, vc_t, diff_lam_q[l], diff_lam_k[l], diff_subln_g[l], diff_table, lam_init)
        x2d = _outproj_ln(x2d, oa.reshape(-1, A_W), ob.reshape(-1, B_Q_W), oc.reshape(-1, C_V_W),
                          w_out_b[l], ln_mix_g[l], ln_mix_b[l], alpha)
        x2d = _ffn_ln(x2d, w_ff1_b[l], w_ff2_b[l], ln_ff_g[l], ln_ff_b[l], alpha)
    return x2d.reshape(bsz, seq, D_MODEL)
```

```python
import functools
import math

import numpy as np
import jax
import jax.numpy as jnp
from jax import lax
from jax.experimental import pallas as pl
from jax.experimental.pallas import tpu as pltpu

D_MODEL = 1024
DEPTH = 2
GRID_W = 64
HEAD_DIM = 64
NA_HEADS = 4
NA_WIN_ROWS = 8
NA_WIN_COLS = 16
SW_HEADS = 8
SW_KV_HEADS = 2
SW_WINDOW = 128
DIFF_HEADS = 4
DIFF_QK_DIM = HEAD_DIM // 2
DIFF_V_DIM = HEAD_DIM
D_FF = 4 * D_MODEL
N_BUCKETS = 32
MAX_DISTANCE = 128
LN_EPS = 1e-5
NEG = -1e30
LOG2E = math.log2(math.e)

A_W = NA_HEADS * HEAD_DIM
B_Q_W = SW_HEADS * HEAD_DIM
B_KV_W = SW_KV_HEADS * HEAD_DIM
C_QK_W = DIFF_HEADS * 2 * DIFF_QK_DIM
C_V_W = DIFF_HEADS * DIFF_V_DIM
IN_SPLITS = (A_W, A_W, A_W, B_Q_W, B_KV_W, B_KV_W, C_QK_W, C_QK_W, C_V_W)
IN_OFFSETS = tuple(int(v) for v in np.cumsum((0,) + IN_SPLITS[:-1]))
IN_WIDTH = sum(IN_SPLITS)
MIX_WIDTH = A_W + B_Q_W + C_V_W
TOKEN_MAJOR_SPLITS = (0, 1, 2, 3, 4, 5, 7)
QC_SPLIT, VC_SPLIT = 6, 8

VMEM_LIMIT_BYTES = 56 * 1024 * 1024

ROW_TILE = 512
NA_ROWS_PER_STEP = 4
SW_TQ = 256
SW_TK = SW_TQ + 2 * SW_WINDOW
DF_TQ = 256
DF_TK = 256
DF_ONES_ROWS = 16
DF_ACC_ROWS = DIFF_V_DIM + DF_ONES_ROWS
DF_NEAR = MAX_DISTANCE // DF_TK + 1
DF_LOOKAHEAD = 5
DF_QM_ROWS = 128
DF_FAR_UNROLL = 4
BIAS_STRIP_ELEMS = 8 * 1024


def _params(semantics):
    return pltpu.CompilerParams(dimension_semantics=semantics, vmem_limit_bytes=VMEM_LIMIT_BYTES)


def _layer_norm_f32(x, g, b):
    mu = jnp.mean(x, axis=-1, keepdims=True)
    xc = x - mu
    var = jnp.mean(xc * xc, axis=-1, keepdims=True)
    return xc * lax.rsqrt(var + LN_EPS) * g + b


def _t5_bucket_np(rel):
    rel = np.asarray(rel, dtype=np.int64)
    nb = N_BUCKETS // 2
    max_exact = nb // 2
    n = np.abs(rel)
    nn = np.maximum(n, 1)
    floor_log2_sq = np.floor(np.log2((nn * nn).astype(np.float64)) + 1e-9).astype(np.int64)
    large = np.minimum(max_exact + floor_log2_sq - 6, nb - 1)
    return (np.where(rel > 0, nb, 0) + np.where(n < max_exact, n, large)).astype(np.int32)


def _na_index_tables():
    n_dr = 2 * NA_WIN_ROWS - 1
    n_dc = 2 * NA_WIN_COLS - 1
    p = np.arange(n_dr - 1)[:, None, None]
    c = np.arange(GRID_W)[None, :, None]
    lane = np.arange(2 * GRID_W)[None, None, :]
    kc = lane % GRID_W
    col_start = np.clip(c - NA_WIN_COLS // 2, 0, GRID_W - NA_WIN_COLS)
    valid = (kc >= col_start) & (kc < col_start + NA_WIN_COLS)
    dc = np.clip(kc - c, -(NA_WIN_COLS - 1), NA_WIN_COLS - 1) + NA_WIN_COLS - 1
    idx = (p + lane // GRID_W) * n_dc + dc
    return np.where(valid, idx, -1).astype(np.int32)


def _sw_index_tables(seq):
    n_steps = seq // SW_TQ
    buckets = []
    for step in (0, 1, n_steps - 1):
        qbase = step * SW_TQ
        kstart = int(np.clip(qbase - SW_WINDOW, 0, seq - SW_TK))
        rel = (kstart + np.arange(SW_TK)[None, :]) - (qbase + np.arange(SW_TQ)[:, None])
        buckets.append(np.where(np.abs(rel) <= SW_WINDOW, _t5_bucket_np(rel), -1))
    return np.stack(buckets).astype(np.int32)


def _df_bucket_tiles():
    tiles = []
    for d in range(-DF_NEAR - 1, DF_NEAR + 2):
        rel = d * DF_TK + np.arange(DF_TK)[:, None] - np.arange(DF_TQ)[None, :]
        tiles.append(_t5_bucket_np(rel))
    tiles = np.stack(tiles)
    assert (tiles[0] == tiles[0, 0, 0]).all() and (tiles[-1] == tiles[-1, 0, 0]).all()
    return tiles


def _bias_lookup_kernel(table_ref, idx_ref, o_ref, *, n_entries):
    h = pl.program_id(1)
    _, rows, cols = idx_ref.shape
    strip_rows = min(rows, max(8, BIAS_STRIP_ELEMS // cols))
    assert rows % strip_rows == 0

    def strip(r, carry):
        r0 = pl.multiple_of(r * strip_rows, strip_rows)
        idx = idx_ref[0, pl.ds(r0, strip_rows), :]
        acc = jnp.full(idx.shape, NEG, jnp.float32)
        for e in range(n_entries):
            acc = jnp.where(idx == e, table_ref[h, e], acc)
        o_ref[0, 0, pl.ds(r0, strip_rows), :] = acc
        return carry

    lax.fori_loop(0, rows // strip_rows, strip, 0)


def _bias_lookup(table, idx, name):
    n_heads, n_entries = table.shape
    n_tiles, rows, cols = idx.shape
    return pl.pallas_call(
        functools.partial(_bias_lookup_kernel, n_entries=n_entries),
        grid=(n_tiles, n_heads),
        in_specs=[pl.BlockSpec(memory_space=pltpu.SMEM),
                  pl.BlockSpec((1, rows, cols), lambda t, h: (t, 0, 0))],
        out_specs=pl.BlockSpec((1, 1, rows, cols), lambda t, h: (t, h, 0, 0)),
        out_shape=jax.ShapeDtypeStruct((n_tiles, n_heads, rows, cols), jnp.float32),
        compiler_params=_params(("arbitrary", "arbitrary")),
        name=name,
    )(table.astype(jnp.float32), jnp.asarray(idx))


def _inproj_kernel(x_ref, g_ref, b_ref, w_ref, wq_t_ref, wv_t_ref, *out_refs, apply_ln):
    x = x_ref[...]
    if apply_ln:
        xn_ref, *out_refs = out_refs
        x = _layer_norm_f32(x, g_ref[...], b_ref[...])
        xn_ref[...] = x
    *proj_refs, qc_t_ref, vc_t_ref = out_refs
    xb = x.astype(jnp.bfloat16)
    for split, o_ref in zip(TOKEN_MAJOR_SPLITS, proj_refs):
        off, width = IN_OFFSETS[split], IN_SPLITS[split]
        o_ref[...] = jnp.dot(xb, w_ref[:, off:off + width],
                             preferred_element_type=jnp.float32).astype(o_ref.dtype)
    nt_dims = (((1,), (1,)), ((), ()))
    qc_t = lax.dot_general(wq_t_ref[...], xb, nt_dims, preferred_element_type=jnp.float32)
    qc_t_ref[0] = (qc_t * (DIFF_QK_DIM ** -0.5 * LOG2E)).astype(qc_t_ref.dtype)
    vc_t = lax.dot_general(wv_t_ref[...], xb, nt_dims, preferred_element_type=jnp.float32)
    vc_t_ref[0] = vc_t.astype(vc_t_ref.dtype)


def _inproj(x2d, g, b, w_bf16, seq, apply_ln):
    n_tok = x2d.shape[0]
    bsz = n_tok // seq
    tiles_per_seq = seq // ROW_TILE
    row = lambda i: (i, 0)
    const = lambda i: (0, 0)
    feat_major = lambda i: (i // tiles_per_seq, 0, i % tiles_per_seq)
    out_shape = [jax.ShapeDtypeStruct((n_tok, IN_SPLITS[s]), jnp.bfloat16) for s in TOKEN_MAJOR_SPLITS]
    out_specs = [pl.BlockSpec((ROW_TILE, IN_SPLITS[s]), row) for s in TOKEN_MAJOR_SPLITS]
    for width in (C_QK_W, C_V_W):
        out_shape.append(jax.ShapeDtypeStruct((bsz, width, seq), jnp.bfloat16))
        out_specs.append(pl.BlockSpec((1, width, ROW_TILE), feat_major))
    if apply_ln:
        out_shape = [jax.ShapeDtypeStruct((n_tok, D_MODEL), jnp.float32)] + out_shape
        out_specs = [pl.BlockSpec((ROW_TILE, D_MODEL), row)] + out_specs
    wq_t = w_bf16[:, IN_OFFSETS[QC_SPLIT]:IN_OFFSETS[QC_SPLIT] + C_QK_W].T
    wv_t = w_bf16[:, IN_OFFSETS[VC_SPLIT]:IN_OFFSETS[VC_SPLIT] + C_V_W].T
    return pl.pallas_call(
        functools.partial(_inproj_kernel, apply_ln=apply_ln),
        grid=(n_tok // ROW_TILE,),
        in_specs=[pl.BlockSpec((ROW_TILE, D_MODEL), row),
                  pl.BlockSpec((1, D_MODEL), const),
                  pl.BlockSpec((1, D_MODEL), const),
                  pl.BlockSpec((D_MODEL, IN_WIDTH), const),
                  pl.BlockSpec((C_QK_W, D_MODEL), const),
                  pl.BlockSpec((C_V_W, D_MODEL), const)],
        out_specs=out_specs,
        out_shape=out_shape,
        compiler_params=_params(("parallel",)),
        name="inproj_ln" if apply_ln else "inproj",
    )(x2d, g.reshape(1, D_MODEL), b.reshape(1, D_MODEL), w_bf16, wq_t, wv_t)


def _na_kernel(q_ref, k_ref, v_ref, t2_ref, o_ref, *, rows):
    i = pl.program_id(1)
    r0 = i * NA_ROWS_PER_STEP
    pair_w = 2 * HEAD_DIM
    win_keys = NA_WIN_ROWS * GRID_W
    low = lax.broadcasted_iota(jnp.int32, (GRID_W, pair_w), 1) < HEAD_DIM
    nt_dims = (((1,), (1,)), ((), ()))
    stages = [(ri, hp) for ri in range(NA_ROWS_PER_STEP) for hp in range(NA_HEADS // 2)]

    def scores(ri, hp):
        r = r0 + ri
        row_start = jnp.clip(r - NA_WIN_ROWS // 2, 0, rows - NA_WIN_ROWS)
        kstart = pl.multiple_of(row_start * GRID_W, GRID_W)
        lanes = slice(hp * pair_w, (hp + 1) * pair_w)
        q_pair = q_ref[0, ri * GRID_W:(ri + 1) * GRID_W, lanes].astype(jnp.float32)
        q_both = jnp.concatenate([jnp.where(low, q_pair, 0.0), jnp.where(low, 0.0, q_pair)], axis=0)
        k_pair = k_ref[0, pl.ds(kstart, win_keys), lanes]
        s = lax.dot_general(q_both.astype(jnp.bfloat16), k_pair, nt_dims,
                            preferred_element_type=jnp.float32)
        p0 = row_start - r + NA_WIN_ROWS - 1
        bias = jnp.concatenate(
            [jnp.concatenate([t2_ref[p0 + 2 * kk, 2 * hp + e] for kk in range(NA_WIN_ROWS // 2)], axis=1)
             for e in range(2)], axis=0)
        return s * (HEAD_DIM ** -0.5 * LOG2E) + bias, kstart

    def finish(ri, hp, s, kstart):
        lanes = slice(hp * pair_w, (hp + 1) * pair_w)
        m = jnp.max(s, axis=-1, keepdims=True)
        p = jnp.exp2(s - m)
        l = jnp.sum(p, axis=-1, keepdims=True)
        v_pair = v_ref[0, pl.ds(kstart, win_keys), lanes]
        o = jnp.dot(p.astype(jnp.bfloat16), v_pair, preferred_element_type=jnp.float32) / l
        o_ref[0, ri * GRID_W:(ri + 1) * GRID_W, lanes] = jnp.where(low, o[:GRID_W], o[GRID_W:]).astype(o_ref.dtype)

    pending = scores(*stages[0])
    for t, (ri, hp) in enumerate(stages):
        current = pending
        if t + 1 < len(stages):
            pending = scores(*stages[t + 1])
        finish(ri, hp, *current)


def _na_attention(q, k, v, rpb):
    bsz, seq, _ = q.shape
    rows = seq // GRID_W
    n_steps = rows // NA_ROWS_PER_STEP
    nq = NA_ROWS_PER_STEP * GRID_W
    idx = _na_index_tables()
    t2 = _bias_lookup(rpb.reshape(NA_HEADS, -1).astype(jnp.float32) * LOG2E, idx, "na_bias")
    return pl.pallas_call(
        functools.partial(_na_kernel, rows=rows),
        grid=(bsz, n_steps),
        in_specs=[pl.BlockSpec((1, nq, A_W), lambda b, i: (b, i, 0)),
                  pl.BlockSpec((1, seq, A_W), lambda b, i: (b, 0, 0)),
                  pl.BlockSpec((1, seq, A_W), lambda b, i: (b, 0, 0)),
                  pl.BlockSpec(t2.shape, lambda b, i: (0, 0, 0, 0))],
        out_specs=pl.BlockSpec((1, nq, A_W), lambda b, i: (b, i, 0)),
        out_shape=jax.ShapeDtypeStruct((bsz, seq, A_W), jnp.bfloat16),
        compiler_params=_params(("parallel", "arbitrary")),
        name="na_attention",
    )(q, k, v, t2)


def _sw_head_order():
    group = SW_HEADS // SW_KV_HEADS
    assert SW_KV_HEADS == 2
    return [h for g in range(group) for h in (g, g + group)]


def _permute_head_blocks(w, axis, start, order):
    def block(lo, hi):
        return lax.slice_in_dim(w, lo, hi, axis=axis)
    end = start + len(order) * HEAD_DIM
    parts = [block(0, start)] + [block(start + h * HEAD_DIM, start + (h + 1) * HEAD_DIM) for h in order]
    parts.append(block(end, w.shape[axis]))
    return jnp.concatenate([p for p in parts if p.shape[axis] > 0], axis=axis)


def _sw_kernel(sink_ref, q_ref, k_ref, v_ref, bias_ref, o_ref, *, seq):
    i = pl.program_id(1)
    kstart = pl.multiple_of(jnp.clip(i * SW_TQ - SW_WINDOW, 0, seq - SW_TK), SW_WINDOW)
    k_win = k_ref[0, pl.ds(kstart, SW_TK), :]
    v_win = v_ref[0, pl.ds(kstart, SW_TK), :]
    group = SW_HEADS // SW_KV_HEADS
    pair_w = 2 * HEAD_DIM
    low = lax.broadcasted_iota(jnp.int32, (SW_TQ, pair_w), 1) < HEAD_DIM
    first = lax.broadcasted_iota(jnp.int32, (2 * SW_TQ, 1), 0) < SW_TQ
    nt_dims = (((1,), (1,)), ((), ()))

    def scores(g):
        q_grp = q_ref[0, :, g * pair_w:(g + 1) * pair_w].astype(jnp.float32)
        q_both = jnp.concatenate([jnp.where(low, q_grp, 0.0), jnp.where(low, 0.0, q_grp)], axis=0)
        s = lax.dot_general(q_both.astype(jnp.bfloat16), k_win, nt_dims,
                            preferred_element_type=jnp.float32)
        bias = jnp.concatenate([bias_ref[0, g], bias_ref[0, g + group]], axis=0)
        return s * (HEAD_DIM ** -0.5 * LOG2E) + bias

    def finish(g, s):
        sink = jnp.where(first, sink_ref[g] * LOG2E, sink_ref[g + group] * LOG2E)
        m = jnp.maximum(jnp.max(s, axis=-1, keepdims=True), sink)
        p = jnp.exp2(s - m)
        l = jnp.sum(p, axis=-1, keepdims=True) + jnp.exp2(sink - m)
        o = jnp.dot(p.astype(jnp.bfloat16), v_win, preferred_element_type=jnp.float32) / l
        o_ref[0, :, g * pair_w:(g + 1) * pair_w] = jnp.where(low, o[:SW_TQ], o[SW_TQ:]).astype(o_ref.dtype)

    pending = scores(0)
    for g in range(group):
        current = pending
        if g + 1 < group:
            pending = scores(g + 1)
        finish(g, current)


def _sw_attention(q, k, v, sink, sw_table):
    bsz, seq, _ = q.shape
    n_steps = seq // SW_TQ
    bias = _bias_lookup(sw_table.T.astype(jnp.float32) * LOG2E, _sw_index_tables(seq), "sw_bias")

    def bias_map(b, i):
        return (jnp.where(i == 0, 0, jnp.where(i == n_steps - 1, 2, 1)), 0, 0, 0)

    return pl.pallas_call(
        functools.partial(_sw_kernel, seq=seq),
        grid=(bsz, n_steps),
        in_specs=[pl.BlockSpec(memory_space=pltpu.SMEM),
                  pl.BlockSpec((1, SW_TQ, B_Q_W), lambda b, i: (b, i, 0)),
                  pl.BlockSpec((1, seq, B_KV_W), lambda b, i: (b, 0, 0)),
                  pl.BlockSpec((1, seq, B_KV_W), lambda b, i: (b, 0, 0)),
                  pl.BlockSpec((1, SW_HEADS, SW_TQ, SW_TK), bias_map)],
        out_specs=pl.BlockSpec((1, SW_TQ, B_Q_W), lambda b, i: (b, i, 0)),
        out_shape=jax.ShapeDtypeStruct((bsz, seq, B_Q_W), jnp.bfloat16),
        compiler_params=_params(("parallel", "arbitrary")),
        name="sw_attention",
    )(sink.astype(jnp.float32), q, k, v, bias)


def _df_kernel(cfar_ref, q_t_ref, k_ref, v_t_ref, bias_ref, lq_ref, lk_ref, g_ref, o_ref,
               qm_scr, s_scr, mx_scr, m_scr, acc_scr, *, lam_init, n_chunks):
    i = pl.program_id(1)
    n_stat = 2 * DIFF_HEADS
    n_near = 2 * DF_NEAR + 2
    n_far = n_chunks - n_near
    per_group = DF_QM_ROWS // DIFF_QK_DIM

    q_t = q_t_ref[0]
    zeros = jnp.zeros((DF_QM_ROWS, DF_TQ), jnp.bfloat16)
    for n in range(n_stat):
        qm_scr[n] = zeros
        dst = (n % per_group) * DIFF_QK_DIM
        qm_scr[n, dst:dst + DIFF_QK_DIM, :] = q_t[n * DIFF_QK_DIM:(n + 1) * DIFF_QK_DIM, :]
    m_scr[...] = jnp.full_like(m_scr, NEG)
    acc_scr[...] = jnp.zeros_like(acc_scr)
    ones = jnp.ones((DF_ONES_ROWS, DF_TK), jnp.bfloat16)

    near0 = jnp.clip(i - DF_NEAR, 0, n_far)

    def near_chunk(u):
        return near0 + u

    def far_chunk(t):
        return t + jnp.where(t >= near0, n_near, 0)

    def issue_scores(j, n, near):
        kstart = pl.multiple_of(j * DF_TK, DF_TK)
        g = n // per_group
        k_grp = k_ref[0, pl.ds(kstart, DF_TK), g * DF_QM_ROWS:(g + 1) * DF_QM_ROWS]
        s = jnp.dot(k_grp, qm_scr[n], preferred_element_type=jnp.float32)
        if near:
            tile = jnp.clip(j - i, -DF_NEAR - 1, DF_NEAR + 1) + DF_NEAR + 1
            s = s + bias_ref[tile, n // 2]
        s_scr[n] = s
        mx_scr[n] = jnp.max(s, axis=0, keepdims=True)

    def consume(j, n, cb):
        kstart = pl.multiple_of(j * DF_TK, DF_TK)
        h = n // 2
        v_aug = jnp.concatenate(
            [v_t_ref[0, h * DIFF_V_DIM:(h + 1) * DIFF_V_DIM, pl.ds(kstart, DF_TK)], ones], axis=0)
        m_prev = m_scr[n]
        m_new = jnp.maximum(m_prev, mx_scr[n] + cb)
        alpha = jnp.exp2(m_prev - m_new)
        p = jnp.exp2(s_scr[n] - (m_new - cb)).astype(jnp.bfloat16)
        acc_scr[n] = alpha * acc_scr[n] + jnp.dot(v_aug, p, preferred_element_type=jnp.float32)
        m_scr[n] = m_new

    def run_chunk(j, j_next, near, next_near, cbs):
        for n in range(n_stat):
            ahead = n + DF_LOOKAHEAD
            if ahead < n_stat:
                issue_scores(j, ahead, near)
            else:
                issue_scores(j_next, ahead - n_stat, next_near)
            consume(j, n, cbs[n // 2])

    for n in range(DF_LOOKAHEAD):
        issue_scores(near_chunk(0), n, True)
    for u in range(n_near):
        last = u == n_near - 1
        run_chunk(near_chunk(u), far_chunk(0) if last else near_chunk(u + 1), True, not last,
                  [0.0] * DIFF_HEADS)

    def far_group(it, carry):
        for u in range(DF_FAR_UNROLL):
            t = DF_FAR_UNROLL * it + u
            j = far_chunk(t)
            j_next = far_chunk(jnp.minimum(t + 1, n_far - 1))
            side = (j > i).astype(jnp.int32)
            run_chunk(j, j_next, False, False, [cfar_ref[side, h] for h in range(DIFF_HEADS)])
        return carry

    lax.fori_loop(0, n_far // DF_FAR_UNROLL, far_group, 0)

    dots = jnp.sum(lq_ref[...] * lk_ref[...], axis=-1, keepdims=True)
    lam = jnp.exp(dots[0:1]) - jnp.exp(dots[1:2]) + lam_init
    outs = []
    for h in range(DIFF_HEADS):
        a1 = acc_scr[2 * h]
        a2 = acc_scr[2 * h + 1]
        o1 = a1[:DIFF_V_DIM] / a1[DIFF_V_DIM:DIFF_V_DIM + 1]
        o2 = a2[:DIFF_V_DIM] / a2[DIFF_V_DIM:DIFF_V_DIM + 1]
        of = o1 - lam * o2
        of = of * lax.rsqrt(jnp.mean(of * of, axis=0, keepdims=True) + LN_EPS)
        outs.append(of * g_ref[...] * (1.0 - lam_init))
    o_ref[0] = jnp.concatenate(outs, axis=0).T.astype(o_ref.dtype)


def _df_attention(q_t, k, v_t, lam_q, lam_k, subln_g, diff_table, lam_init):
    bsz, seq, _ = k.shape
    n_chunks = seq // DF_TK
    assert DF_TQ == DF_TK and DF_LOOKAHEAD < 2 * DIFF_HEADS
    assert (n_chunks - 2 * DF_NEAR - 2) % DF_FAR_UNROLL == 0 and n_chunks > 2 * DF_NEAR + 2
    table_t = diff_table.astype(jnp.float32).T * LOG2E
    bias = _bias_lookup(table_t, _df_bucket_tiles(), "diff_bias")
    cfar = jnp.stack([table_t[:, N_BUCKETS // 2 - 1], table_t[:, N_BUCKETS - 1]])
    n_stat = 2 * DIFF_HEADS
    n_tiles = 2 * DF_NEAR + 3
    const2 = lambda b, i: (0, 0)
    return pl.pallas_call(
        functools.partial(_df_kernel, lam_init=lam_init, n_chunks=n_chunks),
        grid=(bsz, seq // DF_TQ),
        in_specs=[pl.BlockSpec(memory_space=pltpu.SMEM),
                  pl.BlockSpec((1, C_QK_W, DF_TQ), lambda b, i: (b, 0, i)),
                  pl.BlockSpec((1, seq, C_QK_W), lambda b, i: (b, 0, 0)),
                  pl.BlockSpec((1, C_V_W, seq), lambda b, i: (b, 0, 0)),
                  pl.BlockSpec((n_tiles, DIFF_HEADS, DF_TK, DF_TQ), lambda b, i: (0, 0, 0, 0),
                               pipeline_mode=pl.Buffered(1)),
                  pl.BlockSpec((2, DIFF_QK_DIM), const2),
                  pl.BlockSpec((2, DIFF_QK_DIM), const2),
                  pl.BlockSpec((DIFF_V_DIM, 1), const2)],
        out_specs=pl.BlockSpec((1, DF_TQ, C_V_W), lambda b, i: (b, i, 0)),
        out_shape=jax.ShapeDtypeStruct((bsz, seq, C_V_W), jnp.bfloat16),
        scratch_shapes=[pltpu.VMEM((n_stat, DF_QM_ROWS, DF_TQ), jnp.bfloat16),
                        pltpu.VMEM((n_stat, DF_TK, DF_TQ), jnp.float32),
                        pltpu.VMEM((n_stat, 1, DF_TQ), jnp.float32),
                        pltpu.VMEM((n_stat, 1, DF_TQ), jnp.float32),
                        pltpu.VMEM((n_stat, DF_ACC_ROWS, DF_TQ), jnp.float32)],
        compiler_params=_params(("parallel", "arbitrary")),
        name="diff_attention",
    )(cfar, q_t, k, v_t, bias, lam_q.astype(jnp.float32), lam_k.astype(jnp.float32),
      subln_g.astype(jnp.float32).reshape(DIFF_V_DIM, 1))


FF_CHUNK = 1024


def _mix_ffn_kernel(x_ref, oa_ref, ob_ref, oc_ref, wo_ref, g1_ref, b1_ref, w1_ref, w2_ref, g2_ref, b2_ref,
                    o_ref, *, alpha):
    mix = jnp.dot(oa_ref[...], wo_ref[0:A_W, :], preferred_element_type=jnp.float32)
    mix += jnp.dot(ob_ref[...], wo_ref[A_W:A_W + B_Q_W, :], preferred_element_type=jnp.float32)
    mix += jnp.dot(oc_ref[...], wo_ref[A_W + B_Q_W:MIX_WIDTH, :], preferred_element_type=jnp.float32)
    x = _layer_norm_f32(alpha * x_ref[...] + mix, g1_ref[...], b1_ref[...])
    xb = x.astype(jnp.bfloat16)
    y = jnp.zeros((ROW_TILE, D_MODEL), jnp.float32)
    for c in range(D_FF // FF_CHUNK):
        sl = slice(c * FF_CHUNK, (c + 1) * FF_CHUNK)
        h = jnp.maximum(jnp.dot(xb, w1_ref[:, sl], preferred_element_type=jnp.float32), 0.0)
        y += jnp.dot((h * h).astype(jnp.bfloat16), w2_ref[sl, :], preferred_element_type=jnp.float32)
    o_ref[...] = _layer_norm_f32(alpha * x + y, g2_ref[...], b2_ref[...])


def _mix_ffn(x2d, oa, ob, oc, wo_bf16, g1, b1, w1_bf16, w2_bf16, g2, b2, alpha):
    n_tok = x2d.shape[0]
    row = lambda i: (i, 0)
    const = lambda i: (0, 0)
    vec = pl.BlockSpec((1, D_MODEL), const)
    resident = functools.partial(pl.BlockSpec, index_map=const, pipeline_mode=pl.Buffered(1))
    return pl.pallas_call(
        functools.partial(_mix_ffn_kernel, alpha=alpha),
        grid=(n_tok // ROW_TILE,),
        in_specs=[pl.BlockSpec((ROW_TILE, D_MODEL), row),
                  pl.BlockSpec((ROW_TILE, A_W), row),
                  pl.BlockSpec((ROW_TILE, B_Q_W), row),
                  pl.BlockSpec((ROW_TILE, C_V_W), row),
                  resident((MIX_WIDTH, D_MODEL)), vec, vec,
                  resident((D_MODEL, D_FF)), resident((D_FF, D_MODEL)), vec, vec],
        out_specs=pl.BlockSpec((ROW_TILE, D_MODEL), row),
        out_shape=jax.ShapeDtypeStruct((n_tok, D_MODEL), jnp.float32),
        compiler_params=_params(("parallel",)),
        name="mix_ffn",
    )(x2d, oa, ob, oc, wo_bf16, g1.reshape(1, D_MODEL), b1.reshape(1, D_MODEL),
      w1_bf16, w2_bf16, g2.reshape(1, D_MODEL), b2.reshape(1, D_MODEL))


def kernel(x, ln_in_g, ln_in_b, t5_table, w_in, w_out, na_rpb, sw_sink, diff_lam_q, diff_lam_k, diff_subln_g,
           ln_mix_g, ln_mix_b, w_ff1, w_ff2, ln_ff_g, ln_ff_b):
    alpha = (2 * DEPTH) ** 0.25
    bsz, seq, _ = x.shape
    assert x.shape[-1] == D_MODEL and seq % max(ROW_TILE, SW_TQ, DF_TQ, DF_TK) == 0
    assert (seq // GRID_W) % NA_ROWS_PER_STEP == 0 and seq // GRID_W >= NA_WIN_ROWS
    sw_table = t5_table[:, :SW_HEADS]
    diff_table = t5_table[:, SW_HEADS:]
    w_in_b = _permute_head_blocks(w_in.astype(jnp.bfloat16), 2, IN_OFFSETS[3], _sw_head_order())
    w_out_b = _permute_head_blocks(w_out.astype(jnp.bfloat16), 1, A_W, _sw_head_order())
    w_ff1_b = w_ff1.astype(jnp.bfloat16)
    w_ff2_b = w_ff2.astype(jnp.bfloat16)
    x2d = x.reshape(bsz * seq, D_MODEL)
    for l in range(DEPTH):
        lam_init = 0.8 - 0.6 * math.exp(-0.3 * l)
        if l == 0:
            x2d, *proj = _inproj(x2d, ln_in_g, ln_in_b, w_in_b[l], seq, True)
        else:
            proj = _inproj(x2d, ln_in_g, ln_in_b, w_in_b[l], seq, False)
        *tok_major, qc_t, vc_t = proj
        qa, ka, va, qb, kb, vb, kc = [p.reshape(bsz, seq, -1) for p in tok_major]
        oa = _na_attention(qa, ka, va, na_rpb[l])
        ob = _sw_attention(qb, kb, vb, sw_sink[l], sw_table)
        oc = _df_attention(qc_t, kc, vc_t, diff_lam_q[l], diff_lam_k[l], diff_subln_g[l], diff_table, lam_init)
        x2d = _mix_ffn(x2d, oa.reshape(-1, A_W), ob.reshape(-1, B_Q_W), oc.reshape(-1, C_V_W),
                       w_out_b[l], ln_mix_g[l], ln_mix_b[l], w_ff1_b[l], w_ff2_b[l], ln_ff_g[l], ln_ff_b[l], alpha)
    return x2d.reshape(bsz, seq, D_MODEL)
```

```python
import functools
import math

import numpy as np
import jax
import jax.numpy as jnp
from jax import lax
from jax.experimental import pallas as pl
from jax.experimental.pallas import tpu as pltpu

D_MODEL = 1024
DEPTH = 2
GRID_W = 64
HEAD_DIM = 64
NA_HEADS = 4
NA_WIN_ROWS = 8
NA_WIN_COLS = 16
SW_HEADS = 8
SW_KV_HEADS = 2
SW_WINDOW = 128
DIFF_HEADS = 4
DIFF_QK_DIM = HEAD_DIM // 2
DIFF_V_DIM = HEAD_DIM
D_FF = 4 * D_MODEL
N_BUCKETS = 32
MAX_DISTANCE = 128
LN_EPS = 1e-5
NEG = -1e30
LOG2E = math.log2(math.e)

A_W = NA_HEADS * HEAD_DIM
B_Q_W = SW_HEADS * HEAD_DIM
B_KV_W = SW_KV_HEADS * HEAD_DIM
C_QK_W = DIFF_HEADS * 2 * DIFF_QK_DIM
C_V_W = DIFF_HEADS * DIFF_V_DIM
IN_SPLITS = (A_W, A_W, A_W, B_Q_W, B_KV_W, B_KV_W, C_QK_W, C_QK_W, C_V_W)
IN_OFFSETS = tuple(int(v) for v in np.cumsum((0,) + IN_SPLITS[:-1]))
IN_WIDTH = sum(IN_SPLITS)
MIX_WIDTH = A_W + B_Q_W + C_V_W
TOKEN_MAJOR_SPLITS = (0, 1, 2, 3, 4, 5, 7)
QC_SPLIT, VC_SPLIT = 6, 8

VMEM_LIMIT_BYTES = 56 * 1024 * 1024

ROW_TILE = 1024
NA_ROWS_PER_STEP = 4
SW_TQ = 256
SW_TK = SW_TQ + 2 * SW_WINDOW
DF_TQ = 256
DF_TK = 256
DF_ONES_ROWS = 16
DF_ACC_ROWS = DIFF_V_DIM + DF_ONES_ROWS
DF_NEAR = MAX_DISTANCE // DF_TK + 1
DF_LOOKAHEAD = 5
DF_QM_ROWS = 128
DF_FAR_UNROLL = 14
BIAS_STRIP_ELEMS = 8 * 1024


def _params(semantics):
    return pltpu.CompilerParams(dimension_semantics=semantics, vmem_limit_bytes=VMEM_LIMIT_BYTES)


def _layer_norm_f32(x, g, b):
    mu = jnp.mean(x, axis=-1, keepdims=True)
    xc = x - mu
    var = jnp.mean(xc * xc, axis=-1, keepdims=True)
    return xc * lax.rsqrt(var + LN_EPS) * g + b


def _t5_bucket_np(rel):
    rel = np.asarray(rel, dtype=np.int64)
    nb = N_BUCKETS // 2
    max_exact = nb // 2
    n = np.abs(rel)
    nn = np.maximum(n, 1)
    floor_log2_sq = np.floor(np.log2((nn * nn).astype(np.float64)) + 1e-9).astype(np.int64)
    large = np.minimum(max_exact + floor_log2_sq - 6, nb - 1)
    return (np.where(rel > 0, nb, 0) + np.where(n < max_exact, n, large)).astype(np.int32)


def _na_index_tables():
    n_dr = 2 * NA_WIN_ROWS - 1
    n_dc = 2 * NA_WIN_COLS - 1
    p = np.arange(n_dr - 1)[:, None, None]
    c = np.arange(GRID_W)[None, :, None]
    lane = np.arange(2 * GRID_W)[None, None, :]
    kc = lane % GRID_W
    col_start = np.clip(c - NA_WIN_COLS // 2, 0, GRID_W - NA_WIN_COLS)
    valid = (kc >= col_start) & (kc < col_start + NA_WIN_COLS)
    dc = np.clip(kc - c, -(NA_WIN_COLS - 1), NA_WIN_COLS - 1) + NA_WIN_COLS - 1
    idx = (p + lane // GRID_W) * n_dc + dc
    return np.where(valid, idx, -1).astype(np.int32)


def _sw_index_tables(seq):
    n_steps = seq // SW_TQ
    buckets = []
    for step in (0, 1, n_steps - 1):
        qbase = step * SW_TQ
        kstart = int(np.clip(qbase - SW_WINDOW, 0, seq - SW_TK))
        rel = (kstart + np.arange(SW_TK)[None, :]) - (qbase + np.arange(SW_TQ)[:, None])
        buckets.append(np.where(np.abs(rel) <= SW_WINDOW, _t5_bucket_np(rel), -1))
    return np.stack(buckets).astype(np.int32)


def _df_bucket_tiles():
    tiles = []
    for d in range(-DF_NEAR - 1, DF_NEAR + 2):
        rel = d * DF_TK + np.arange(DF_TK)[:, None] - np.arange(DF_TQ)[None, :]
        tiles.append(_t5_bucket_np(rel))
    tiles = np.stack(tiles)
    assert (tiles[0] == tiles[0, 0, 0]).all() and (tiles[-1] == tiles[-1, 0, 0]).all()
    return tiles


def _bias_lookup_kernel(base_ref, table_ref, idx_ref, o_ref, *, span):
    base = base_ref[pl.program_id(0)]
    h = pl.program_id(1)
    _, rows, cols = idx_ref.shape
    strip_rows = min(rows, max(8, BIAS_STRIP_ELEMS // cols))
    assert rows % strip_rows == 0

    def strip(r, carry):
        r0 = pl.multiple_of(r * strip_rows, strip_rows)
        idx = idx_ref[0, pl.ds(r0, strip_rows), :]
        acc = jnp.full(idx.shape, NEG, jnp.float32)
        for e in range(span):
            acc = jnp.where(idx == base + e, table_ref[h, base + e], acc)
        o_ref[0, 0, pl.ds(r0, strip_rows), :] = acc
        return carry

    lax.fori_loop(0, rows // strip_rows, strip, 0)


def _bias_lookup(table, idx, name):
    n_heads, n_entries = table.shape
    n_tiles, rows, cols = idx.shape
    flat = idx.reshape(n_tiles, -1)
    lo = np.where(flat >= 0, flat, n_entries).min(axis=1)
    span = int((flat.max(axis=1) - lo).max()) + 1
    bases = np.minimum(lo, n_entries - span).astype(np.int32)
    return pl.pallas_call(
        functools.partial(_bias_lookup_kernel, span=span),
        grid=(n_tiles, n_heads),
        in_specs=[pl.BlockSpec(memory_space=pltpu.SMEM),
                  pl.BlockSpec(memory_space=pltpu.SMEM),
                  pl.BlockSpec((1, rows, cols), lambda t, h: (t, 0, 0))],
        out_specs=pl.BlockSpec((1, 1, rows, cols), lambda t, h: (t, h, 0, 0)),
        out_shape=jax.ShapeDtypeStruct((n_tiles, n_heads, rows, cols), jnp.float32),
        compiler_params=_params(("arbitrary", "arbitrary")),
        name=name,
    )(jnp.asarray(bases), table.astype(jnp.float32), jnp.asarray(idx))


def _inproj_kernel(x_ref, g_ref, b_ref, w_ref, wq_t_ref, wv_t_ref, *out_refs, apply_ln):
    x = x_ref[...]
    if apply_ln:
        xn_ref, *out_refs = out_refs
        x = _layer_norm_f32(x, g_ref[...], b_ref[...])
        xn_ref[...] = x
    *proj_refs, qc_t_ref, vc_t_ref = out_refs
    xb = x.astype(jnp.bfloat16)
    for split, o_ref in zip(TOKEN_MAJOR_SPLITS, proj_refs):
        off, width = IN_OFFSETS[split], IN_SPLITS[split]
        o_ref[...] = jnp.dot(xb, w_ref[:, off:off + width],
                             preferred_element_type=jnp.float32).astype(o_ref.dtype)
    nt_dims = (((1,), (1,)), ((), ()))
    qc_t = lax.dot_general(wq_t_ref[...], xb, nt_dims, preferred_element_type=jnp.float32)
    qc_t_ref[0] = (qc_t * (DIFF_QK_DIM ** -0.5 * LOG2E)).astype(qc_t_ref.dtype)
    vc_t = lax.dot_general(wv_t_ref[...], xb, nt_dims, preferred_element_type=jnp.float32)
    vc_t_ref[0] = vc_t.astype(vc_t_ref.dtype)


def _inproj(x2d, g, b, w_bf16, seq, apply_ln):
    n_tok = x2d.shape[0]
    bsz = n_tok // seq
    tiles_per_seq = seq // ROW_TILE
    row = lambda i: (i, 0)
    const = lambda i: (0, 0)
    feat_major = lambda i: (i // tiles_per_seq, 0, i % tiles_per_seq)
    out_shape = [jax.ShapeDtypeStruct((n_tok, IN_SPLITS[s]), jnp.bfloat16) for s in TOKEN_MAJOR_SPLITS]
    out_specs = [pl.BlockSpec((ROW_TILE, IN_SPLITS[s]), row) for s in TOKEN_MAJOR_SPLITS]
    for width in (C_QK_W, C_V_W):
        out_shape.append(jax.ShapeDtypeStruct((bsz, width, seq), jnp.bfloat16))
        out_specs.append(pl.BlockSpec((1, width, ROW_TILE), feat_major))
    if apply_ln:
        out_shape = [jax.ShapeDtypeStruct((n_tok, D_MODEL), jnp.float32)] + out_shape
        out_specs = [pl.BlockSpec((ROW_TILE, D_MODEL), row)] + out_specs
    wq_t = w_bf16[:, IN_OFFSETS[QC_SPLIT]:IN_OFFSETS[QC_SPLIT] + C_QK_W].T
    wv_t = w_bf16[:, IN_OFFSETS[VC_SPLIT]:IN_OFFSETS[VC_SPLIT] + C_V_W].T
    return pl.pallas_call(
        functools.partial(_inproj_kernel, apply_ln=apply_ln),
        grid=(n_tok // ROW_TILE,),
        in_specs=[pl.BlockSpec((ROW_TILE, D_MODEL), row),
                  pl.BlockSpec((1, D_MODEL), const),
                  pl.BlockSpec((1, D_MODEL), const),
                  pl.BlockSpec((D_MODEL, IN_WIDTH), const),
                  pl.BlockSpec((C_QK_W, D_MODEL), const),
                  pl.BlockSpec((C_V_W, D_MODEL), const)],
        out_specs=out_specs,
        out_shape=out_shape,
        compiler_params=_params(("parallel",)),
        name="inproj_ln" if apply_ln else "inproj",
    )(x2d, g.reshape(1, D_MODEL), b.reshape(1, D_MODEL), w_bf16, wq_t, wv_t)


def _na_kernel(q_ref, k_ref, v_ref, t2_ref, o_ref, *, rows):
    i = pl.program_id(1)
    r0 = i * NA_ROWS_PER_STEP
    pair_w = 2 * HEAD_DIM
    win_keys = NA_WIN_ROWS * GRID_W
    low = lax.broadcasted_iota(jnp.int32, (GRID_W, pair_w), 1) < HEAD_DIM
    nt_dims = (((1,), (1,)), ((), ()))
    stages = [(ri, hp) for ri in range(NA_ROWS_PER_STEP) for hp in range(NA_HEADS // 2)]

    def scores(ri, hp):
        r = r0 + ri
        row_start = jnp.clip(r - NA_WIN_ROWS // 2, 0, rows - NA_WIN_ROWS)
        kstart = pl.multiple_of(row_start * GRID_W, GRID_W)
        lanes = slice(hp * pair_w, (hp + 1) * pair_w)
        q_pair = q_ref[0, ri * GRID_W:(ri + 1) * GRID_W, lanes].astype(jnp.float32)
        q_both = jnp.concatenate([jnp.where(low, q_pair, 0.0), jnp.where(low, 0.0, q_pair)], axis=0)
        k_pair = k_ref[0, pl.ds(kstart, win_keys), lanes]
        s = lax.dot_general(q_both.astype(jnp.bfloat16), k_pair, nt_dims,
                            preferred_element_type=jnp.float32)
        p0 = row_start - r + NA_WIN_ROWS - 1
        bias = jnp.concatenate(
            [jnp.concatenate([t2_ref[p0 + 2 * kk, 2 * hp + e] for kk in range(NA_WIN_ROWS // 2)], axis=1)
             for e in range(2)], axis=0)
        return s * (HEAD_DIM ** -0.5 * LOG2E) + bias, kstart

    def finish(ri, hp, s, kstart):
        lanes = slice(hp * pair_w, (hp + 1) * pair_w)
        m = jnp.max(s, axis=-1, keepdims=True)
        p = jnp.exp2(s - m)
        l = jnp.sum(p, axis=-1, keepdims=True)
        v_pair = v_ref[0, pl.ds(kstart, win_keys), lanes]
        o = jnp.dot(p.astype(jnp.bfloat16), v_pair, preferred_element_type=jnp.float32) / l
        o_ref[0, ri * GRID_W:(ri + 1) * GRID_W, lanes] = jnp.where(low, o[:GRID_W], o[GRID_W:]).astype(o_ref.dtype)

    pending = scores(*stages[0])
    for t, (ri, hp) in enumerate(stages):
        current = pending
        if t + 1 < len(stages):
            pending = scores(*stages[t + 1])
        finish(ri, hp, *current)


def _na_attention(q, k, v, rpb):
    bsz, seq, _ = q.shape
    rows = seq // GRID_W
    n_steps = rows // NA_ROWS_PER_STEP
    nq = NA_ROWS_PER_STEP * GRID_W
    idx = _na_index_tables()
    t2 = _bias_lookup(rpb.reshape(NA_HEADS, -1).astype(jnp.float32) * LOG2E, idx, "na_bias")
    return pl.pallas_call(
        functools.partial(_na_kernel, rows=rows),
        grid=(bsz, n_steps),
        in_specs=[pl.BlockSpec((1, nq, A_W), lambda b, i: (b, i, 0)),
                  pl.BlockSpec((1, seq, A_W), lambda b, i: (b, 0, 0)),
                  pl.BlockSpec((1, seq, A_W), lambda b, i: (b, 0, 0)),
                  pl.BlockSpec(t2.shape, lambda b, i: (0, 0, 0, 0))],
        out_specs=pl.BlockSpec((1, nq, A_W), lambda b, i: (b, i, 0)),
        out_shape=jax.ShapeDtypeStruct((bsz, seq, A_W), jnp.bfloat16),
        compiler_params=_params(("parallel", "arbitrary")),
        name="na_attention",
    )(q, k, v, t2)


def _sw_head_order():
    group = SW_HEADS // SW_KV_HEADS
    assert SW_KV_HEADS == 2
    return [h for g in range(group) for h in (g, g + group)]


def _permute_head_blocks(w, axis, start, order):
    def block(lo, hi):
        return lax.slice_in_dim(w, lo, hi, axis=axis)
    end = start + len(order) * HEAD_DIM
    parts = [block(0, start)] + [block(start + h * HEAD_DIM, start + (h + 1) * HEAD_DIM) for h in order]
    parts.append(block(end, w.shape[axis]))
    return jnp.concatenate([p for p in parts if p.shape[axis] > 0], axis=axis)


def _sw_kernel(sink_ref, q_ref, k_ref, v_ref, bias_ref, o_ref, *, seq):
    i = pl.program_id(1)
    kstart = pl.multiple_of(jnp.clip(i * SW_TQ - SW_WINDOW, 0, seq - SW_TK), SW_WINDOW)
    k_win = k_ref[0, pl.ds(kstart, SW_TK), :]
    v_win = v_ref[0, pl.ds(kstart, SW_TK), :]
    group = SW_HEADS // SW_KV_HEADS
    pair_w = 2 * HEAD_DIM
    low = lax.broadcasted_iota(jnp.int32, (SW_TQ, pair_w), 1) < HEAD_DIM
    first = lax.broadcasted_iota(jnp.int32, (2 * SW_TQ, 1), 0) < SW_TQ
    nt_dims = (((1,), (1,)), ((), ()))

    def scores(g):
        q_grp = q_ref[0, :, g * pair_w:(g + 1) * pair_w].astype(jnp.float32)
        q_both = jnp.concatenate([jnp.where(low, q_grp, 0.0), jnp.where(low, 0.0, q_grp)], axis=0)
        s = lax.dot_general(q_both.astype(jnp.bfloat16), k_win, nt_dims,
                            preferred_element_type=jnp.float32)
        bias = jnp.concatenate([bias_ref[0, g], bias_ref[0, g + group]], axis=0)
        return s * (HEAD_DIM ** -0.5 * LOG2E) + bias

    def finish(g, s):
        sink = jnp.where(first, sink_ref[g] * LOG2E, sink_ref[g + group] * LOG2E)
        m = jnp.maximum(jnp.max(s, axis=-1, keepdims=True), sink)
        p = jnp.exp2(s - m)
        l = jnp.sum(p, axis=-1, keepdims=True) + jnp.exp2(sink - m)
        o = jnp.dot(p.astype(jnp.bfloat16), v_win, preferred_element_type=jnp.float32) / l
        o_ref[0, :, g * pair_w:(g + 1) * pair_w] = jnp.where(low, o[:SW_TQ], o[SW_TQ:]).astype(o_ref.dtype)

    pending = scores(0)
    for g in range(group):
        current = pending
        if g + 1 < group:
            pending = scores(g + 1)
        finish(g, current)


def _sw_attention(q, k, v, sink, sw_table):
    bsz, seq, _ = q.shape
    n_steps = seq // SW_TQ
    bias = _bias_lookup(sw_table.T.astype(jnp.float32) * LOG2E, _sw_index_tables(seq), "sw_bias")

    def bias_map(b, i):
        return (jnp.where(i == 0, 0, jnp.where(i == n_steps - 1, 2, 1)), 0, 0, 0)

    return pl.pallas_call(
        functools.partial(_sw_kernel, seq=seq),
        grid=(bsz, n_steps),
        in_specs=[pl.BlockSpec(memory_space=pltpu.SMEM),
                  pl.BlockSpec((1, SW_TQ, B_Q_W), lambda b, i: (b, i, 0)),
                  pl.BlockSpec((1, seq, B_KV_W), lambda b, i: (b, 0, 0)),
                  pl.BlockSpec((1, seq, B_KV_W), lambda b, i: (b, 0, 0)),
                  pl.BlockSpec((1, SW_HEADS, SW_TQ, SW_TK), bias_map)],
        out_specs=pl.BlockSpec((1, SW_TQ, B_Q_W), lambda b, i: (b, i, 0)),
        out_shape=jax.ShapeDtypeStruct((bsz, seq, B_Q_W), jnp.bfloat16),
        compiler_params=_params(("parallel", "arbitrary")),
        name="sw_attention",
    )(sink.astype(jnp.float32), q, k, v, bias)


def _df_kernel(cfar_ref, q_t_ref, k_ref, v_t_ref, bias_ref, lq_ref, lk_ref, g_ref, o_ref,
               qm_scr, s_scr, mx_scr, m_scr, acc_scr, *, lam_init, n_chunks):
    i = pl.program_id(1)
    n_stat = 2 * DIFF_HEADS
    n_near = 2 * DF_NEAR + 2
    n_far = n_chunks - n_near
    per_group = DF_QM_ROWS // DIFF_QK_DIM

    q_t = q_t_ref[0]
    zeros = jnp.zeros((DF_QM_ROWS, DF_TQ), jnp.bfloat16)
    for n in range(n_stat):
        qm_scr[n] = zeros
        dst = (n % per_group) * DIFF_QK_DIM
        qm_scr[n, dst:dst + DIFF_QK_DIM, :] = q_t[n * DIFF_QK_DIM:(n + 1) * DIFF_QK_DIM, :]
    m_scr[...] = jnp.full_like(m_scr, NEG)
    acc_scr[...] = jnp.zeros_like(acc_scr)
    ones = jnp.ones((DF_ONES_ROWS, DF_TK), jnp.bfloat16)

    near0 = jnp.clip(i - DF_NEAR, 0, n_far)

    def near_chunk(u):
        return near0 + u

    def far_chunk(t):
        return t + jnp.where(t >= near0, n_near, 0)

    def issue_scores(j, n, near):
        kstart = pl.multiple_of(j * DF_TK, DF_TK)
        g = n // per_group
        k_grp = k_ref[0, pl.ds(kstart, DF_TK), g * DF_QM_ROWS:(g + 1) * DF_QM_ROWS]
        s = jnp.dot(k_grp, qm_scr[n], preferred_element_type=jnp.float32)
        if near:
            tile = jnp.clip(j - i, -DF_NEAR - 1, DF_NEAR + 1) + DF_NEAR + 1
            s = s + bias_ref[tile, n // 2]
        s_scr[n] = s
        mx_scr[n] = jnp.max(s, axis=0, keepdims=True)

    def consume(j, n, cb):
        kstart = pl.multiple_of(j * DF_TK, DF_TK)
        h = n // 2
        v_aug = jnp.concatenate(
            [v_t_ref[0, h * DIFF_V_DIM:(h + 1) * DIFF_V_DIM, pl.ds(kstart, DF_TK)], ones], axis=0)
        m_prev = m_scr[n]
        m_new = jnp.maximum(m_prev, mx_scr[n] + cb)
        alpha = jnp.exp2(m_prev - m_new)
        p = jnp.exp2(s_scr[n] - (m_new - cb)).astype(jnp.bfloat16)
        acc_scr[n] = alpha * acc_scr[n] + jnp.dot(v_aug, p, preferred_element_type=jnp.float32)
        m_scr[n] = m_new

    def run_chunk(j, j_next, near, next_near, cbs):
        for n in range(n_stat):
            ahead = n + DF_LOOKAHEAD
            if ahead < n_stat:
                issue_scores(j, ahead, near)
            else:
                issue_scores(j_next, ahead - n_stat, next_near)
            consume(j, n, cbs[n // 2])

    for n in range(DF_LOOKAHEAD):
        issue_scores(near_chunk(0), n, True)
    for u in range(n_near):
        last = u == n_near - 1
        run_chunk(near_chunk(u), far_chunk(0) if last else near_chunk(u + 1), True, not last,
                  [0.0] * DIFF_HEADS)

    def far_group(it, carry):
        for u in range(DF_FAR_UNROLL):
            t = DF_FAR_UNROLL * it + u
            j = far_chunk(t)
            j_next = far_chunk(jnp.minimum(t + 1, n_far - 1))
            side = (j > i).astype(jnp.int32)
            run_chunk(j, j_next, False, False, [cfar_ref[side, h] for h in range(DIFF_HEADS)])
        return carry

    lax.fori_loop(0, n_far // DF_FAR_UNROLL, far_group, 0)

    dots = jnp.sum(lq_ref[...] * lk_ref[...], axis=-1, keepdims=True)
    lam = jnp.exp(dots[0:1]) - jnp.exp(dots[1:2]) + lam_init
    outs = []
    for h in range(DIFF_HEADS):
        a1 = acc_scr[2 * h]
        a2 = acc_scr[2 * h + 1]
        o1 = a1[:DIFF_V_DIM] / a1[DIFF_V_DIM:DIFF_V_DIM + 1]
        o2 = a2[:DIFF_V_DIM] / a2[DIFF_V_DIM:DIFF_V_DIM + 1]
        of = o1 - lam * o2
        of = of * lax.rsqrt(jnp.mean(of * of, axis=0, keepdims=True) + LN_EPS)
        outs.append(of * g_ref[...] * (1.0 - lam_init))
    o_ref[0] = jnp.concatenate(outs, axis=0).T.astype(o_ref.dtype)


def _df_attention(q_t, k, v_t, lam_q, lam_k, subln_g, diff_table, lam_init):
    bsz, seq, _ = k.shape
    n_chunks = seq // DF_TK
    assert DF_TQ == DF_TK and DF_LOOKAHEAD < 2 * DIFF_HEADS
    assert (n_chunks - 2 * DF_NEAR - 2) % DF_FAR_UNROLL == 0 and n_chunks > 2 * DF_NEAR + 2
    table_t = diff_table.astype(jnp.float32).T * LOG2E
    bias = _bias_lookup(table_t, _df_bucket_tiles(), "diff_bias")
    cfar = jnp.stack([table_t[:, N_BUCKETS // 2 - 1], table_t[:, N_BUCKETS - 1]])
    n_stat = 2 * DIFF_HEADS
    n_tiles = 2 * DF_NEAR + 3
    const2 = lambda b, i: (0, 0)
    return pl.pallas_call(
        functools.partial(_df_kernel, lam_init=lam_init, n_chunks=n_chunks),
        grid=(bsz, seq // DF_TQ),
        in_specs=[pl.BlockSpec(memory_space=pltpu.SMEM),
                  pl.BlockSpec((1, C_QK_W, DF_TQ), lambda b, i: (b, 0, i)),
                  pl.BlockSpec((1, seq, C_QK_W), lambda b, i: (b, 0, 0)),
                  pl.BlockSpec((1, C_V_W, seq), lambda b, i: (b, 0, 0)),
                  pl.BlockSpec((n_tiles, DIFF_HEADS, DF_TK, DF_TQ), lambda b, i: (0, 0, 0, 0),
                               pipeline_mode=pl.Buffered(1)),
                  pl.BlockSpec((2, DIFF_QK_DIM), const2),
                  pl.BlockSpec((2, DIFF_QK_DIM), const2),
                  pl.BlockSpec((DIFF_V_DIM, 1), const2)],
        out_specs=pl.BlockSpec((1, DF_TQ, C_V_W), lambda b, i: (b, i, 0)),
        out_shape=jax.ShapeDtypeStruct((bsz, seq, C_V_W), jnp.bfloat16),
        scratch_shapes=[pltpu.VMEM((n_stat, DF_QM_ROWS, DF_TQ), jnp.bfloat16),
                        pltpu.VMEM((n_stat, DF_TK, DF_TQ), jnp.float32),
                        pltpu.VMEM((n_stat, 1, DF_TQ), jnp.float32),
                        pltpu.VMEM((n_stat, 1, DF_TQ), jnp.float32),
                        pltpu.VMEM((n_stat, DF_ACC_ROWS, DF_TQ), jnp.float32)],
        compiler_params=_params(("parallel", "arbitrary")),
        name="diff_attention",
    )(cfar, q_t, k, v_t, bias, lam_q.astype(jnp.float32), lam_k.astype(jnp.float32),
      subln_g.astype(jnp.float32).reshape(DIFF_V_DIM, 1))


FF_CHUNK = 1024
MIX_SUBTILE = 256


def _mix_ffn_kernel(x_ref, oa_ref, ob_ref, oc_ref, wo_ref, g1_ref, b1_ref, w1_ref, w2_ref, g2_ref, b2_ref,
                    o_ref, *, alpha):
    subtiles = [slice(r, r + MIX_SUBTILE) for r in range(0, ROW_TILE, MIX_SUBTILE)]

    def mix(rows):
        acc = jnp.dot(oa_ref[rows, :], wo_ref[0:A_W, :], preferred_element_type=jnp.float32)
        acc += jnp.dot(ob_ref[rows, :], wo_ref[A_W:A_W + B_Q_W, :], preferred_element_type=jnp.float32)
        acc += jnp.dot(oc_ref[rows, :], wo_ref[A_W + B_Q_W:MIX_WIDTH, :], preferred_element_type=jnp.float32)
        return alpha * x_ref[rows, :] + acc

    mixed = [mix(rows) for rows in subtiles]
    xs = [_layer_norm_f32(v, g1_ref[...], b1_ref[...]) for v in mixed]
    for rows, x in zip(subtiles, xs):
        xb = x.astype(jnp.bfloat16)
        y = jnp.zeros((MIX_SUBTILE, D_MODEL), jnp.float32)
        for c in range(D_FF // FF_CHUNK):
            sl = slice(c * FF_CHUNK, (c + 1) * FF_CHUNK)
            h = jnp.maximum(jnp.dot(xb, w1_ref[:, sl], preferred_element_type=jnp.float32), 0.0)
            y += jnp.dot((h * h).astype(jnp.bfloat16), w2_ref[sl, :], preferred_element_type=jnp.float32)
        o_ref[rows, :] = _layer_norm_f32(alpha * x + y, g2_ref[...], b2_ref[...])


def _mix_ffn(x2d, oa, ob, oc, wo_bf16, g1, b1, w1_bf16, w2_bf16, g2, b2, alpha):
    n_tok = x2d.shape[0]
    row = lambda i: (i, 0)
    const = lambda i: (0, 0)
    vec = pl.BlockSpec((1, D_MODEL), const)
    resident = functools.partial(pl.BlockSpec, index_map=const, pipeline_mode=pl.Buffered(1))
    return pl.pallas_call(
        functools.partial(_mix_ffn_kernel, alpha=alpha),
        grid=(n_tok // ROW_TILE,),
        in_specs=[pl.BlockSpec((ROW_TILE, D_MODEL), row),
                  pl.BlockSpec((ROW_TILE, A_W), row),
                  pl.BlockSpec((ROW_TILE, B_Q_W), row),
                  pl.BlockSpec((ROW_TILE, C_V_W), row),
                  resident((MIX_WIDTH, D_MODEL)), vec, vec,
                  resident((D_MODEL, D_FF)), resident((D_FF, D_MODEL)), vec, vec],
        out_specs=pl.BlockSpec((ROW_TILE, D_MODEL), row),
        out_shape=jax.ShapeDtypeStruct((n_tok, D_MODEL), jnp.float32),
        compiler_params=_params(("parallel",)),
        name="mix_ffn",
    )(x2d, oa, ob, oc, wo_bf16, g1.reshape(1, D_MODEL), b1.reshape(1, D_MODEL),
      w1_bf16, w2_bf16, g2.reshape(1, D_MODEL), b2.reshape(1, D_MODEL))


def kernel(x, ln_in_g, ln_in_b, t5_table, w_in, w_out, na_rpb, sw_sink, diff_lam_q, diff_lam_k, diff_subln_g,
           ln_mix_g, ln_mix_b, w_ff1, w_ff2, ln_ff_g, ln_ff_b):
    alpha = (2 * DEPTH) ** 0.25
    bsz, seq, _ = x.shape
    assert x.shape[-1] == D_MODEL and seq % max(ROW_TILE, SW_TQ, DF_TQ, DF_TK) == 0
    assert (seq // GRID_W) % NA_ROWS_PER_STEP == 0 and seq // GRID_W >= NA_WIN_ROWS
    sw_table = t5_table[:, :SW_HEADS]
    diff_table = t5_table[:, SW_HEADS:]
    w_in_b = _permute_head_blocks(w_in.astype(jnp.bfloat16), 2, IN_OFFSETS[3], _sw_head_order())
    w_out_b = _permute_head_blocks(w_out.astype(jnp.bfloat16), 1, A_W, _sw_head_order())
    w_ff1_b = w_ff1.astype(jnp.bfloat16)
    w_ff2_b = w_ff2.astype(jnp.bfloat16)
    x2d = x.reshape(bsz * seq, D_MODEL)
    for l in range(DEPTH):
        lam_init = 0.8 - 0.6 * math.exp(-0.3 * l)
        if l == 0:
            x2d, *proj = _inproj(x2d, ln_in_g, ln_in_b, w_in_b[l], seq, True)
        else:
            proj = _inproj(x2d, ln_in_g, ln_in_b, w_in_b[l], seq, False)
        *tok_major, qc_t, vc_t = proj
        qa, ka, va, qb, kb, vb, kc = [p.reshape(bsz, seq, -1) for p in tok_major]
        oa = _na_attention(qa, ka, va, na_rpb[l])
        ob = _sw_attention(qb, kb, vb, sw_sink[l], sw_table)
        oc = _df_attention(qc_t, kc, vc_t, diff_lam_q[l], diff_lam_k[l], diff_subln_g[l], diff_table, lam_init)
        x2d = _mix_ffn(x2d, oa.reshape(-1, A_W), ob.reshape(-1, B_Q_W), oc.reshape(-1, C_V_W),
                       w_out_b[l], ln_mix_g[l], ln_mix_b[l], w_ff1_b[l], w_ff2_b[l], ln_ff_g[l], ln_ff_b[l], alpha)
    return x2d.reshape(bsz, seq, D_MODEL)
```

```python
import functools
import math

import numpy as np
import jax
import jax.numpy as jnp
from jax import lax
from jax.experimental import pallas as pl
from jax.experimental.pallas import tpu as pltpu

D_MODEL = 1024
DEPTH = 2
GRID_W = 64
HEAD_DIM = 64
NA_HEADS = 4
NA_WIN_ROWS = 8
NA_WIN_COLS = 16
SW_HEADS = 8
SW_KV_HEADS = 2
SW_WINDOW = 128
DIFF_HEADS = 4
DIFF_QK_DIM = HEAD_DIM // 2
DIFF_V_DIM = HEAD_DIM
D_FF = 4 * D_MODEL
N_BUCKETS = 32
MAX_DISTANCE = 128
LN_EPS = 1e-5
NEG = -1e30
LOG2E = math.log2(math.e)

A_W = NA_HEADS * HEAD_DIM
B_Q_W = SW_HEADS * HEAD_DIM
B_KV_W = SW_KV_HEADS * HEAD_DIM
C_QK_W = DIFF_HEADS * 2 * DIFF_QK_DIM
C_V_W = DIFF_HEADS * DIFF_V_DIM
IN_SPLITS = (A_W, A_W, A_W, B_Q_W, B_KV_W, B_KV_W, C_QK_W, C_QK_W, C_V_W)
IN_OFFSETS = tuple(int(v) for v in np.cumsum((0,) + IN_SPLITS[:-1]))
IN_WIDTH = sum(IN_SPLITS)
MIX_WIDTH = A_W + B_Q_W + C_V_W
TOKEN_MAJOR_SPLITS = (0, 1, 2, 3, 4, 5, 7)
QC_SPLIT, VC_SPLIT = 6, 8

VMEM_LIMIT_BYTES = 56 * 1024 * 1024

ROW_TILE = 1024
NA_ROWS_PER_STEP = 16
ATTN_LOOKAHEAD = 2
SW_TQ = 256
SW_TK = SW_TQ + 2 * SW_WINDOW
SW_TILES_PER_STEP = 4
DF_TQ = 256
DF_TK = 256
DF_ONES_ROWS = 16
DF_ACC_ROWS = DIFF_V_DIM + DF_ONES_ROWS
DF_NEAR = MAX_DISTANCE // DF_TK + 1
DF_LOOKAHEAD = 5
DF_QM_ROWS = 128
DF_FAR_UNROLL = 14
BIAS_STRIP_ELEMS = 8 * 1024


def _params(semantics):
    return pltpu.CompilerParams(dimension_semantics=semantics, vmem_limit_bytes=VMEM_LIMIT_BYTES)


def _layer_norm_f32(x, g, b):
    mu = jnp.mean(x, axis=-1, keepdims=True)
    xc = x - mu
    var = jnp.mean(xc * xc, axis=-1, keepdims=True)
    return xc * lax.rsqrt(var + LN_EPS) * g + b


def _t5_bucket_np(rel):
    rel = np.asarray(rel, dtype=np.int64)
    nb = N_BUCKETS // 2
    max_exact = nb // 2
    n = np.abs(rel)
    nn = np.maximum(n, 1)
    floor_log2_sq = np.floor(np.log2((nn * nn).astype(np.float64)) + 1e-9).astype(np.int64)
    large = np.minimum(max_exact + floor_log2_sq - 6, nb - 1)
    return (np.where(rel > 0, nb, 0) + np.where(n < max_exact, n, large)).astype(np.int32)


def _na_index_tables():
    n_dr = 2 * NA_WIN_ROWS - 1
    n_dc = 2 * NA_WIN_COLS - 1
    p = np.arange(n_dr - 1)[:, None, None]
    c = np.arange(GRID_W)[None, :, None]
    lane = np.arange(2 * GRID_W)[None, None, :]
    kc = lane % GRID_W
    col_start = np.clip(c - NA_WIN_COLS // 2, 0, GRID_W - NA_WIN_COLS)
    valid = (kc >= col_start) & (kc < col_start + NA_WIN_COLS)
    dc = np.clip(kc - c, -(NA_WIN_COLS - 1), NA_WIN_COLS - 1) + NA_WIN_COLS - 1
    idx = (p + lane // GRID_W) * n_dc + dc
    return np.where(valid, idx, -1).astype(np.int32)


def _sw_index_tables(seq):
    n_steps = seq // SW_TQ
    buckets = []
    for step in (0, 1, n_steps - 1):
        qbase = step * SW_TQ
        kstart = int(np.clip(qbase - SW_WINDOW, 0, seq - SW_TK))
        rel = (kstart + np.arange(SW_TK)[None, :]) - (qbase + np.arange(SW_TQ)[:, None])
        buckets.append(np.where(np.abs(rel) <= SW_WINDOW, _t5_bucket_np(rel), -1))
    return np.stack(buckets).astype(np.int32)


def _df_bucket_tiles():
    tiles = []
    for d in range(-DF_NEAR - 1, DF_NEAR + 2):
        rel = d * DF_TK + np.arange(DF_TK)[:, None] - np.arange(DF_TQ)[None, :]
        tiles.append(_t5_bucket_np(rel))
    tiles = np.stack(tiles)
    assert (tiles[0] == tiles[0, 0, 0]).all() and (tiles[-1] == tiles[-1, 0, 0]).all()
    return tiles


def _bias_lookup_kernel(base_ref, table_ref, idx_ref, o_ref, *, span):
    base = base_ref[pl.program_id(0)]
    h = pl.program_id(1)
    _, rows, cols = idx_ref.shape
    strip_rows = min(rows, max(8, BIAS_STRIP_ELEMS // cols))
    assert rows % strip_rows == 0

    def strip(r, carry):
        r0 = pl.multiple_of(r * strip_rows, strip_rows)
        idx = idx_ref[0, pl.ds(r0, strip_rows), :]
        acc = jnp.full(idx.shape, NEG, jnp.float32)
        for e in range(span):
            acc = jnp.where(idx == base + e, table_ref[h, base + e], acc)
        o_ref[0, 0, pl.ds(r0, strip_rows), :] = acc
        return carry

    lax.fori_loop(0, rows // strip_rows, strip, 0)


def _bias_lookup(table, idx, name):
    n_heads, n_entries = table.shape
    n_tiles, rows, cols = idx.shape
    flat = idx.reshape(n_tiles, -1)
    lo = np.where(flat >= 0, flat, n_entries).min(axis=1)
    span = int((flat.max(axis=1) - lo).max()) + 1
    bases = np.minimum(lo, n_entries - span).astype(np.int32)
    return pl.pallas_call(
        functools.partial(_bias_lookup_kernel, span=span),
        grid=(n_tiles, n_heads),
        in_specs=[pl.BlockSpec(memory_space=pltpu.SMEM),
                  pl.BlockSpec(memory_space=pltpu.SMEM),
                  pl.BlockSpec((1, rows, cols), lambda t, h: (t, 0, 0))],
        out_specs=pl.BlockSpec((1, 1, rows, cols), lambda t, h: (t, h, 0, 0)),
        out_shape=jax.ShapeDtypeStruct((n_tiles, n_heads, rows, cols), jnp.float32),
        compiler_params=_params(("arbitrary", "arbitrary")),
        name=name,
    )(jnp.asarray(bases), table.astype(jnp.float32), jnp.asarray(idx))


def _inproj_kernel(x_ref, g_ref, b_ref, w_ref, wq_t_ref, wv_t_ref, *out_refs, apply_ln):
    x = x_ref[...]
    if apply_ln:
        xn_ref, *out_refs = out_refs
        x = _layer_norm_f32(x, g_ref[...], b_ref[...])
        xn_ref[...] = x
    *proj_refs, qc_t_ref, vc_t_ref = out_refs
    xb = x.astype(jnp.bfloat16)
    for split, o_ref in zip(TOKEN_MAJOR_SPLITS, proj_refs):
        off, width = IN_OFFSETS[split], IN_SPLITS[split]
        o_ref[...] = jnp.dot(xb, w_ref[:, off:off + width],
                             preferred_element_type=jnp.float32).astype(o_ref.dtype)
    nt_dims = (((1,), (1,)), ((), ()))
    qc_t = lax.dot_general(wq_t_ref[...], xb, nt_dims, preferred_element_type=jnp.float32)
    qc_t_ref[0] = (qc_t * (DIFF_QK_DIM ** -0.5 * LOG2E)).astype(qc_t_ref.dtype)
    vc_t = lax.dot_general(wv_t_ref[...], xb, nt_dims, preferred_element_type=jnp.float32)
    vc_t_ref[0] = vc_t.astype(vc_t_ref.dtype)


def _inproj(x2d, g, b, w_bf16, seq, apply_ln):
    n_tok = x2d.shape[0]
    bsz = n_tok // seq
    tiles_per_seq = seq // ROW_TILE
    row = lambda i: (i, 0)
    const = lambda i: (0, 0)
    feat_major = lambda i: (i // tiles_per_seq, 0, i % tiles_per_seq)
    out_shape = [jax.ShapeDtypeStruct((n_tok, IN_SPLITS[s]), jnp.bfloat16) for s in TOKEN_MAJOR_SPLITS]
    out_specs = [pl.BlockSpec((ROW_TILE, IN_SPLITS[s]), row) for s in TOKEN_MAJOR_SPLITS]
    for width in (C_QK_W, C_V_W):
        out_shape.append(jax.ShapeDtypeStruct((bsz, width, seq), jnp.bfloat16))
        out_specs.append(pl.BlockSpec((1, width, ROW_TILE), feat_major))
    if apply_ln:
        out_shape = [jax.ShapeDtypeStruct((n_tok, D_MODEL), jnp.float32)] + out_shape
        out_specs = [pl.BlockSpec((ROW_TILE, D_MODEL), row)] + out_specs
    wq_t = w_bf16[:, IN_OFFSETS[QC_SPLIT]:IN_OFFSETS[QC_SPLIT] + C_QK_W].T
    wv_t = w_bf16[:, IN_OFFSETS[VC_SPLIT]:IN_OFFSETS[VC_SPLIT] + C_V_W].T
    return pl.pallas_call(
        functools.partial(_inproj_kernel, apply_ln=apply_ln),
        grid=(n_tok // ROW_TILE,),
        in_specs=[pl.BlockSpec((ROW_TILE, D_MODEL), row),
                  pl.BlockSpec((1, D_MODEL), const),
                  pl.BlockSpec((1, D_MODEL), const),
                  pl.BlockSpec((D_MODEL, IN_WIDTH), const),
                  pl.BlockSpec((C_QK_W, D_MODEL), const),
                  pl.BlockSpec((C_V_W, D_MODEL), const)],
        out_specs=out_specs,
        out_shape=out_shape,
        compiler_params=_params(("parallel",)),
        name="inproj_ln" if apply_ln else "inproj",
    )(x2d, g.reshape(1, D_MODEL), b.reshape(1, D_MODEL), w_bf16, wq_t, wv_t)


def _na_kernel(q_ref, k_ref, v_ref, t2_ref, o_ref, *, rows):
    i = pl.program_id(1)
    r0 = i * NA_ROWS_PER_STEP
    pair_w = 2 * HEAD_DIM
    win_keys = NA_WIN_ROWS * GRID_W
    low = lax.broadcasted_iota(jnp.int32, (GRID_W, pair_w), 1) < HEAD_DIM
    nt_dims = (((1,), (1,)), ((), ()))
    stages = [(ri, hp) for ri in range(NA_ROWS_PER_STEP) for hp in range(NA_HEADS // 2)]

    def scores(ri, hp):
        r = r0 + ri
        row_start = jnp.clip(r - NA_WIN_ROWS // 2, 0, rows - NA_WIN_ROWS)
        kstart = pl.multiple_of(row_start * GRID_W, GRID_W)
        lanes = slice(hp * pair_w, (hp + 1) * pair_w)
        q_pair = q_ref[0, ri * GRID_W:(ri + 1) * GRID_W, lanes].astype(jnp.float32)
        q_both = jnp.concatenate([jnp.where(low, q_pair, 0.0), jnp.where(low, 0.0, q_pair)], axis=0)
        k_pair = k_ref[0, pl.ds(kstart, win_keys), lanes]
        s = lax.dot_general(q_both.astype(jnp.bfloat16), k_pair, nt_dims,
                            preferred_element_type=jnp.float32)
        p0 = row_start - r + NA_WIN_ROWS - 1
        bias = jnp.concatenate(
            [jnp.concatenate([t2_ref[p0 + 2 * kk, 2 * hp + e] for kk in range(NA_WIN_ROWS // 2)], axis=1)
             for e in range(2)], axis=0)
        return s + bias, kstart

    def finish(ri, hp, s, kstart):
        lanes = slice(hp * pair_w, (hp + 1) * pair_w)
        m = jnp.max(s, axis=-1, keepdims=True)
        p = jnp.exp2(s - m)
        l = jnp.sum(p, axis=-1, keepdims=True)
        v_pair = v_ref[0, pl.ds(kstart, win_keys), lanes]
        o = jnp.dot(p.astype(jnp.bfloat16), v_pair, preferred_element_type=jnp.float32) / l
        o_ref[0, ri * GRID_W:(ri + 1) * GRID_W, lanes] = jnp.where(low, o[:GRID_W], o[GRID_W:]).astype(o_ref.dtype)

    pending = [scores(*st) for st in stages[:ATTN_LOOKAHEAD]]
    for t, (ri, hp) in enumerate(stages):
        current = pending.pop(0)
        if t + ATTN_LOOKAHEAD < len(stages):
            pending.append(scores(*stages[t + ATTN_LOOKAHEAD]))
        finish(ri, hp, *current)


def _na_attention(q, k, v, rpb):
    bsz, seq, _ = q.shape
    rows = seq // GRID_W
    n_steps = rows // NA_ROWS_PER_STEP
    nq = NA_ROWS_PER_STEP * GRID_W
    idx = _na_index_tables()
    t2 = _bias_lookup(rpb.reshape(NA_HEADS, -1).astype(jnp.float32) * LOG2E, idx, "na_bias")
    return pl.pallas_call(
        functools.partial(_na_kernel, rows=rows),
        grid=(bsz, n_steps),
        in_specs=[pl.BlockSpec((1, nq, A_W), lambda b, i: (b, i, 0)),
                  pl.BlockSpec((1, seq, A_W), lambda b, i: (b, 0, 0)),
                  pl.BlockSpec((1, seq, A_W), lambda b, i: (b, 0, 0)),
                  pl.BlockSpec(t2.shape, lambda b, i: (0, 0, 0, 0))],
        out_specs=pl.BlockSpec((1, nq, A_W), lambda b, i: (b, i, 0)),
        out_shape=jax.ShapeDtypeStruct((bsz, seq, A_W), jnp.bfloat16),
        compiler_params=_params(("parallel", "arbitrary")),
        name="na_attention",
    )(q, k, v, t2)


def _sw_head_order():
    group = SW_HEADS // SW_KV_HEADS
    assert SW_KV_HEADS == 2
    return [h for g in range(group) for h in (g, g + group)]


def _permute_head_blocks(w, axis, start, order):
    def block(lo, hi):
        return lax.slice_in_dim(w, lo, hi, axis=axis)
    end = start + len(order) * HEAD_DIM
    parts = [block(0, start)] + [block(start + h * HEAD_DIM, start + (h + 1) * HEAD_DIM) for h in order]
    parts.append(block(end, w.shape[axis]))
    return jnp.concatenate([p for p in parts if p.shape[axis] > 0], axis=axis)


def _sw_kernel(sink_ref, q_ref, k_ref, v_ref, bias_ref, o_ref, *, seq):
    i = pl.program_id(1)
    n_tiles = seq // SW_TQ
    group = SW_HEADS // SW_KV_HEADS
    pair_w = 2 * HEAD_DIM
    low = lax.broadcasted_iota(jnp.int32, (SW_TQ, pair_w), 1) < HEAD_DIM
    nt_dims = (((1,), (1,)), ((), ()))
    stages = [(tt, g, e) for tt in range(SW_TILES_PER_STEP) for g in range(group) for e in range(SW_KV_HEADS)]

    def window(tt):
        tile = i * SW_TILES_PER_STEP + tt
        kstart = pl.multiple_of(jnp.clip(tile * SW_TQ - SW_WINDOW, 0, seq - SW_TK), SW_WINDOW)
        variant = jnp.where(tile == 0, 0, jnp.where(tile == n_tiles - 1, 2, 1))
        return kstart, variant

    def scores(tt, g, e):
        kstart, variant = window(tt)
        q_grp = q_ref[0, tt * SW_TQ:(tt + 1) * SW_TQ, g * pair_w:(g + 1) * pair_w].astype(jnp.float32)
        q_head = jnp.where(low, q_grp, 0.0) if e == 0 else jnp.where(low, 0.0, q_grp)
        s = lax.dot_general(q_head.astype(jnp.bfloat16), k_ref[0, pl.ds(kstart, SW_TK), :], nt_dims,
                            preferred_element_type=jnp.float32)
        return s + bias_ref[variant, g + e * group]

    def attend(tt, g, e, s):
        kstart, _ = window(tt)
        sink = sink_ref[g + e * group] * LOG2E
        m = jnp.maximum(jnp.max(s, axis=-1, keepdims=True), sink)
        p = jnp.exp2(s - m)
        l = jnp.sum(p, axis=-1, keepdims=True) + jnp.exp2(sink - m)
        return jnp.dot(p.astype(jnp.bfloat16), v_ref[0, pl.ds(kstart, SW_TK), :],
                       preferred_element_type=jnp.float32) / l

    pending = [scores(*st) for st in stages[:ATTN_LOOKAHEAD]]
    o_low = None
    for t, (tt, g, e) in enumerate(stages):
        current = pending.pop(0)
        if t + ATTN_LOOKAHEAD < len(stages):
            pending.append(scores(*stages[t + ATTN_LOOKAHEAD]))
        o = attend(tt, g, e, current)
        if e == 0:
            o_low = o
        else:
            o_ref[0, tt * SW_TQ:(tt + 1) * SW_TQ, g * pair_w:(g + 1) * pair_w] = (
                jnp.where(low, o_low, o).astype(o_ref.dtype))


def _sw_attention(q, k, v, sink, sw_table):
    bsz, seq, _ = q.shape
    step_q = SW_TILES_PER_STEP * SW_TQ
    assert seq % step_q == 0
    bias = _bias_lookup(sw_table.T.astype(jnp.float32) * LOG2E, _sw_index_tables(seq), "sw_bias")
    return pl.pallas_call(
        functools.partial(_sw_kernel, seq=seq),
        grid=(bsz, seq // step_q),
        in_specs=[pl.BlockSpec(memory_space=pltpu.SMEM),
                  pl.BlockSpec((1, step_q, B_Q_W), lambda b, i: (b, i, 0)),
                  pl.BlockSpec((1, seq, B_KV_W), lambda b, i: (b, 0, 0)),
                  pl.BlockSpec((1, seq, B_KV_W), lambda b, i: (b, 0, 0)),
                  pl.BlockSpec(bias.shape, lambda b, i: (0, 0, 0, 0), pipeline_mode=pl.Buffered(1))],
        out_specs=pl.BlockSpec((1, step_q, B_Q_W), lambda b, i: (b, i, 0)),
        out_shape=jax.ShapeDtypeStruct((bsz, seq, B_Q_W), jnp.bfloat16),
        compiler_params=_params(("parallel", "arbitrary")),
        name="sw_attention",
    )(sink.astype(jnp.float32), q, k, v, bias)


def _df_kernel(cfar_ref, q_t_ref, k_ref, v_t_ref, bias_ref, lq_ref, lk_ref, g_ref, o_ref,
               qm_scr, s_scr, mx_scr, m_scr, acc_scr, *, lam_init, n_chunks):
    i = pl.program_id(1)
    n_stat = 2 * DIFF_HEADS
    n_near = 2 * DF_NEAR + 2
    n_far = n_chunks - n_near
    per_group = DF_QM_ROWS // DIFF_QK_DIM

    q_t = q_t_ref[0]
    zeros = jnp.zeros((DF_QM_ROWS, DF_TQ), jnp.bfloat16)
    for n in range(n_stat):
        qm_scr[n] = zeros
        dst = (n % per_group) * DIFF_QK_DIM
        qm_scr[n, dst:dst + DIFF_QK_DIM, :] = q_t[n * DIFF_QK_DIM:(n + 1) * DIFF_QK_DIM, :]
    m_scr[...] = jnp.full_like(m_scr, NEG)
    acc_scr[...] = jnp.zeros_like(acc_scr)
    ones = jnp.ones((DF_ONES_ROWS, DF_TK), jnp.bfloat16)

    near0 = jnp.clip(i - DF_NEAR, 0, n_far)

    def near_chunk(u):
        return near0 + u

    def far_chunk(t):
        return t + jnp.where(t >= near0, n_near, 0)

    def issue_scores(j, n, near):
        kstart = pl.multiple_of(j * DF_TK, DF_TK)
        g = n // per_group
        k_grp = k_ref[0, pl.ds(kstart, DF_TK), g * DF_QM_ROWS:(g + 1) * DF_QM_ROWS]
        s = jnp.dot(k_grp, qm_scr[n], preferred_element_type=jnp.float32)
        if near:
            tile = jnp.clip(j - i, -DF_NEAR - 1, DF_NEAR + 1) + DF_NEAR + 1
            s = s + bias_ref[tile, n // 2]
        s_scr[n] = s
        mx_scr[n] = jnp.max(s, axis=0, keepdims=True)

    def consume(j, n, cb):
        kstart = pl.multiple_of(j * DF_TK, DF_TK)
        h = n // 2
        v_aug = jnp.concatenate(
            [v_t_ref[0, h * DIFF_V_DIM:(h + 1) * DIFF_V_DIM, pl.ds(kstart, DF_TK)], ones], axis=0)
        m_prev = m_scr[n]
        m_new = jnp.maximum(m_prev, mx_scr[n] + cb)
        alpha = jnp.exp2(m_prev - m_new)
        p = jnp.exp2(s_scr[n] - (m_new - cb)).astype(jnp.bfloat16)
        acc_scr[n] = alpha * acc_scr[n] + jnp.dot(v_aug, p, preferred_element_type=jnp.float32)
        m_scr[n] = m_new

    def run_chunk(j, j_next, near, next_near, cbs):
        for n in range(n_stat):
            ahead = n + DF_LOOKAHEAD
            if ahead < n_stat:
                issue_scores(j, ahead, near)
            else:
                issue_scores(j_next, ahead - n_stat, next_near)
            consume(j, n, cbs[n // 2])

    for n in range(DF_LOOKAHEAD):
        issue_scores(near_chunk(0), n, True)
    for u in range(n_near):
        last = u == n_near - 1
        run_chunk(near_chunk(u), far_chunk(0) if last else near_chunk(u + 1), True, not last,
                  [0.0] * DIFF_HEADS)

    def far_group(it, carry):
        for u in range(DF_FAR_UNROLL):
            t = DF_FAR_UNROLL * it + u
            j = far_chunk(t)
            j_next = far_chunk(jnp.minimum(t + 1, n_far - 1))
            side = (j > i).astype(jnp.int32)
            run_chunk(j, j_next, False, False, [cfar_ref[side, h] for h in range(DIFF_HEADS)])
        return carry

    lax.fori_loop(0, n_far // DF_FAR_UNROLL, far_group, 0)

    dots = jnp.sum(lq_ref[...] * lk_ref[...], axis=-1, keepdims=True)
    lam = jnp.exp(dots[0:1]) - jnp.exp(dots[1:2]) + lam_init
    outs = []
    for h in range(DIFF_HEADS):
        a1 = acc_scr[2 * h]
        a2 = acc_scr[2 * h + 1]
        o1 = a1[:DIFF_V_DIM] / a1[DIFF_V_DIM:DIFF_V_DIM + 1]
        o2 = a2[:DIFF_V_DIM] / a2[DIFF_V_DIM:DIFF_V_DIM + 1]
        of = o1 - lam * o2
        of = of * lax.rsqrt(jnp.mean(of * of, axis=0, keepdims=True) + LN_EPS)
        outs.append(of * g_ref[...] * (1.0 - lam_init))
    o_ref[0] = jnp.concatenate(outs, axis=0).T.astype(o_ref.dtype)


def _df_attention(q_t, k, v_t, lam_q, lam_k, subln_g, diff_table, lam_init):
    bsz, seq, _ = k.shape
    n_chunks = seq // DF_TK
    assert DF_TQ == DF_TK and DF_LOOKAHEAD < 2 * DIFF_HEADS
    assert (n_chunks - 2 * DF_NEAR - 2) % DF_FAR_UNROLL == 0 and n_chunks > 2 * DF_NEAR + 2
    table_t = diff_table.astype(jnp.float32).T * LOG2E
    bias = _bias_lookup(table_t, _df_bucket_tiles(), "diff_bias")
    cfar = jnp.stack([table_t[:, N_BUCKETS // 2 - 1], table_t[:, N_BUCKETS - 1]])
    n_stat = 2 * DIFF_HEADS
    n_tiles = 2 * DF_NEAR + 3
    const2 = lambda b, i: (0, 0)
    return pl.pallas_call(
        functools.partial(_df_kernel, lam_init=lam_init, n_chunks=n_chunks),
        grid=(bsz, seq // DF_TQ),
        in_specs=[pl.BlockSpec(memory_space=pltpu.SMEM),
                  pl.BlockSpec((1, C_QK_W, DF_TQ), lambda b, i: (b, 0, i)),
                  pl.BlockSpec((1, seq, C_QK_W), lambda b, i: (b, 0, 0)),
                  pl.BlockSpec((1, C_V_W, seq), lambda b, i: (b, 0, 0)),
                  pl.BlockSpec((n_tiles, DIFF_HEADS, DF_TK, DF_TQ), lambda b, i: (0, 0, 0, 0),
                               pipeline_mode=pl.Buffered(1)),
                  pl.BlockSpec((2, DIFF_QK_DIM), const2),
                  pl.BlockSpec((2, DIFF_QK_DIM), const2),
                  pl.BlockSpec((DIFF_V_DIM, 1), const2)],
        out_specs=pl.BlockSpec((1, DF_TQ, C_V_W), lambda b, i: (b, i, 0)),
        out_shape=jax.ShapeDtypeStruct((bsz, seq, C_V_W), jnp.bfloat16),
        scratch_shapes=[pltpu.VMEM((n_stat, DF_QM_ROWS, DF_TQ), jnp.bfloat16),
                        pltpu.VMEM((n_stat, DF_TK, DF_TQ), jnp.float32),
                        pltpu.VMEM((n_stat, 1, DF_TQ), jnp.float32),
                        pltpu.VMEM((n_stat, 1, DF_TQ), jnp.float32),
                        pltpu.VMEM((n_stat, DF_ACC_ROWS, DF_TQ), jnp.float32)],
        compiler_params=_params(("parallel", "arbitrary")),
        name="diff_attention",
    )(cfar, q_t, k, v_t, bias, lam_q.astype(jnp.float32), lam_k.astype(jnp.float32),
      subln_g.astype(jnp.float32).reshape(DIFF_V_DIM, 1))


FF_CHUNK = 1024
MIX_SUBTILE = 256


def _mix_ffn_kernel(x_ref, oa_ref, ob_ref, oc_ref, wo_ref, g1_ref, b1_ref, w1_ref, w2_ref, g2_ref, b2_ref,
                    o_ref, *, alpha):
    subtiles = [slice(r, r + MIX_SUBTILE) for r in range(0, ROW_TILE, MIX_SUBTILE)]

    def mix(rows):
        acc = jnp.dot(oa_ref[rows, :], wo_ref[0:A_W, :], preferred_element_type=jnp.float32)
        acc += jnp.dot(ob_ref[rows, :], wo_ref[A_W:A_W + B_Q_W, :], preferred_element_type=jnp.float32)
        acc += jnp.dot(oc_ref[rows, :], wo_ref[A_W + B_Q_W:MIX_WIDTH, :], preferred_element_type=jnp.float32)
        return alpha * x_ref[rows, :] + acc

    mixed = [mix(rows) for rows in subtiles]
    xs = [_layer_norm_f32(v, g1_ref[...], b1_ref[...]) for v in mixed]
    for rows, x in zip(subtiles, xs):
        xb = x.astype(jnp.bfloat16)
        y = jnp.zeros((MIX_SUBTILE, D_MODEL), jnp.float32)
        for c in range(D_FF // FF_CHUNK):
            sl = slice(c * FF_CHUNK, (c + 1) * FF_CHUNK)
            h = jnp.maximum(jnp.dot(xb, w1_ref[:, sl], preferred_element_type=jnp.float32), 0.0)
            y += jnp.dot((h * h).astype(jnp.bfloat16), w2_ref[sl, :], preferred_element_type=jnp.float32)
        o_ref[rows, :] = _layer_norm_f32(alpha * x + y, g2_ref[...], b2_ref[...])


def _mix_ffn(x2d, oa, ob, oc, wo_bf16, g1, b1, w1_bf16, w2_bf16, g2, b2, alpha):
    n_tok = x2d.shape[0]
    row = lambda i: (i, 0)
    const = lambda i: (0, 0)
    vec = pl.BlockSpec((1, D_MODEL), const)
    resident = functools.partial(pl.BlockSpec, index_map=const, pipeline_mode=pl.Buffered(1))
    return pl.pallas_call(
        functools.partial(_mix_ffn_kernel, alpha=alpha),
        grid=(n_tok // ROW_TILE,),
        in_specs=[pl.BlockSpec((ROW_TILE, D_MODEL), row),
                  pl.BlockSpec((ROW_TILE, A_W), row),
                  pl.BlockSpec((ROW_TILE, B_Q_W), row),
                  pl.BlockSpec((ROW_TILE, C_V_W), row),
                  resident((MIX_WIDTH, D_MODEL)), vec, vec,
                  resident((D_MODEL, D_FF)), resident((D_FF, D_MODEL)), vec, vec],
        out_specs=pl.BlockSpec((ROW_TILE, D_MODEL), row),
        out_shape=jax.ShapeDtypeStruct((n_tok, D_MODEL), jnp.float32),
        compiler_params=_params(("parallel",)),
        name="mix_ffn",
    )(x2d, oa, ob, oc, wo_bf16, g1.reshape(1, D_MODEL), b1.reshape(1, D_MODEL),
      w1_bf16, w2_bf16, g2.reshape(1, D_MODEL), b2.reshape(1, D_MODEL))


def kernel(x, ln_in_g, ln_in_b, t5_table, w_in, w_out, na_rpb, sw_sink, diff_lam_q, diff_lam_k, diff_subln_g,
           ln_mix_g, ln_mix_b, w_ff1, w_ff2, ln_ff_g, ln_ff_b):
    alpha = (2 * DEPTH) ** 0.25
    bsz, seq, _ = x.shape
    assert x.shape[-1] == D_MODEL and seq % max(ROW_TILE, SW_TQ, DF_TQ, DF_TK) == 0
    assert (seq // GRID_W) % NA_ROWS_PER_STEP == 0 and seq // GRID_W >= NA_WIN_ROWS
    sw_table = t5_table[:, :SW_HEADS]
    diff_table = t5_table[:, SW_HEADS:]
    col_scale = np.ones((IN_WIDTH,), np.float32)
    for split in (0, 3):
        col_scale[IN_OFFSETS[split]:IN_OFFSETS[split] + IN_SPLITS[split]] = HEAD_DIM ** -0.5 * LOG2E
    w_in_b = _permute_head_blocks((w_in * col_scale).astype(jnp.bfloat16), 2, IN_OFFSETS[3], _sw_head_order())
    w_out_b = _permute_head_blocks(w_out.astype(jnp.bfloat16), 1, A_W, _sw_head_order())
    w_ff1_b = w_ff1.astype(jnp.bfloat16)
    w_ff2_b = w_ff2.astype(jnp.bfloat16)
    x2d = x.reshape(bsz * seq, D_MODEL)
    for l in range(DEPTH):
        lam_init = 0.8 - 0.6 * math.exp(-0.3 * l)
        if l == 0:
            x2d, *proj = _inproj(x2d, ln_in_g, ln_in_b, w_in_b[l], seq, True)
        else:
            proj = _inproj(x2d, ln_in_g, ln_in_b, w_in_b[l], seq, False)
        *tok_major, qc_t, vc_t = proj
        qa, ka, va, qb, kb, vb, kc = [p.reshape(bsz, seq, -1) for p in tok_major]
        oa = _na_attention(qa, ka, va, na_rpb[l])
        ob = _sw_attention(qb, kb, vb, sw_sink[l], sw_table)
        oc = _df_attention(qc_t, kc, vc_t, diff_lam_q[l], diff_lam_k[l], diff_subln_g[l], diff_table, lam_init)
        x2d = _mix_ffn(x2d, oa.reshape(-1, A_W), ob.reshape(-1, B_Q_W), oc.reshape(-1, C_V_W),
                       w_out_b[l], ln_mix_g[l], ln_mix_b[l], w_ff1_b[l], w_ff2_b[l], ln_ff_g[l], ln_ff_b[l], alpha)
    return x2d.reshape(bsz, seq, D_MODEL)
```

```python
import functools
import math

import numpy as np
import jax
import jax.numpy as jnp
from jax import lax
from jax.experimental import pallas as pl
from jax.experimental.pallas import tpu as pltpu

D_MODEL = 1024
DEPTH = 2
GRID_W = 64
HEAD_DIM = 64
NA_HEADS = 4
NA_WIN_ROWS = 8
NA_WIN_COLS = 16
SW_HEADS = 8
SW_KV_HEADS = 2
SW_WINDOW = 128
DIFF_HEADS = 4
DIFF_QK_DIM = HEAD_DIM // 2
DIFF_V_DIM = HEAD_DIM
D_FF = 4 * D_MODEL
N_BUCKETS = 32
MAX_DISTANCE = 128
LN_EPS = 1e-5
NEG = -1e30
LOG2E = math.log2(math.e)

A_W = NA_HEADS * HEAD_DIM
B_Q_W = SW_HEADS * HEAD_DIM
B_KV_W = SW_KV_HEADS * HEAD_DIM
C_QK_W = DIFF_HEADS * 2 * DIFF_QK_DIM
C_V_W = DIFF_HEADS * DIFF_V_DIM
IN_SPLITS = (A_W, A_W, A_W, B_Q_W, B_KV_W, B_KV_W, C_QK_W, C_QK_W, C_V_W)
IN_OFFSETS = tuple(int(v) for v in np.cumsum((0,) + IN_SPLITS[:-1]))
IN_WIDTH = sum(IN_SPLITS)
MIX_WIDTH = A_W + B_Q_W + C_V_W
TOKEN_MAJOR_SPLITS = (0, 1, 2, 3, 4, 5, 7)
QC_SPLIT, VC_SPLIT = 6, 8

VMEM_LIMIT_BYTES = 56 * 1024 * 1024

ROW_TILE = 1024
NA_ROWS_PER_STEP = 16
ATTN_LOOKAHEAD = 2
SW_TQ = 256
SW_TK = SW_TQ + 2 * SW_WINDOW
SW_TILES_PER_STEP = 4
DF_TQ = 256
DF_TK = 256
DF_ONES_ROWS = 16
DF_ACC_ROWS = DIFF_V_DIM + DF_ONES_ROWS
DF_NEAR = MAX_DISTANCE // DF_TK + 1
DF_LOOKAHEAD = 5
DF_QM_ROWS = 128
DF_FAR_UNROLL = 14
BIAS_STRIP_ELEMS = 8 * 1024


def _params(semantics):
    return pltpu.CompilerParams(dimension_semantics=semantics, vmem_limit_bytes=VMEM_LIMIT_BYTES)


def _layer_norm_f32(x, g, b):
    mu = jnp.mean(x, axis=-1, keepdims=True)
    xc = x - mu
    var = jnp.mean(xc * xc, axis=-1, keepdims=True)
    return xc * lax.rsqrt(var + LN_EPS) * g + b


def _t5_bucket_np(rel):
    rel = np.asarray(rel, dtype=np.int64)
    nb = N_BUCKETS // 2
    max_exact = nb // 2
    n = np.abs(rel)
    nn = np.maximum(n, 1)
    floor_log2_sq = np.floor(np.log2((nn * nn).astype(np.float64)) + 1e-9).astype(np.int64)
    large = np.minimum(max_exact + floor_log2_sq - 6, nb - 1)
    return (np.where(rel > 0, nb, 0) + np.where(n < max_exact, n, large)).astype(np.int32)


def _na_index_tables():
    n_dr = 2 * NA_WIN_ROWS - 1
    n_dc = 2 * NA_WIN_COLS - 1
    p = np.arange(n_dr - 1)[:, None, None]
    c = np.arange(GRID_W)[None, :, None]
    lane = np.arange(2 * GRID_W)[None, None, :]
    kc = lane % GRID_W
    col_start = np.clip(c - NA_WIN_COLS // 2, 0, GRID_W - NA_WIN_COLS)
    valid = (kc >= col_start) & (kc < col_start + NA_WIN_COLS)
    dc = np.clip(kc - c, -(NA_WIN_COLS - 1), NA_WIN_COLS - 1) + NA_WIN_COLS - 1
    idx = (p + lane // GRID_W) * n_dc + dc
    return np.where(valid, idx, -1).astype(np.int32)


def _sw_index_tables():
    rel = np.arange(SW_TK + 2 * SW_WINDOW)[None, :] - np.arange(SW_TQ)[:, None] - 2 * SW_WINDOW
    return np.where(np.abs(rel) <= SW_WINDOW, _t5_bucket_np(rel), -1).astype(np.int32)[None]


def _df_bucket_tiles():
    tiles = []
    for d in range(-DF_NEAR - 1, DF_NEAR + 2):
        rel = d * DF_TK + np.arange(DF_TK)[:, None] - np.arange(DF_TQ)[None, :]
        tiles.append(_t5_bucket_np(rel))
    tiles = np.stack(tiles)
    assert (tiles[0] == tiles[0, 0, 0]).all() and (tiles[-1] == tiles[-1, 0, 0]).all()
    return tiles


def _bias_lookup_kernel(base_ref, table_ref, idx_ref, o_ref, *, span):
    base = base_ref[pl.program_id(0)]
    h = pl.program_id(1)
    _, rows, cols = idx_ref.shape
    strip_rows = min(rows, max(8, BIAS_STRIP_ELEMS // cols // 8 * 8))
    assert rows % strip_rows == 0

    def strip(r, carry):
        r0 = pl.multiple_of(r * strip_rows, strip_rows)
        idx = idx_ref[0, pl.ds(r0, strip_rows), :]
        acc = jnp.full(idx.shape, NEG, jnp.float32)
        for e in range(span):
            acc = jnp.where(idx == base + e, table_ref[h, base + e], acc)
        o_ref[0, 0, pl.ds(r0, strip_rows), :] = acc
        return carry

    lax.fori_loop(0, rows // strip_rows, strip, 0)


def _bias_lookup(table, idx, name):
    n_heads, n_entries = table.shape
    n_tiles, rows, cols = idx.shape
    flat = idx.reshape(n_tiles, -1)
    lo = np.where(flat >= 0, flat, n_entries).min(axis=1)
    span = int((flat.max(axis=1) - lo).max()) + 1
    bases = np.minimum(lo, n_entries - span).astype(np.int32)
    return pl.pallas_call(
        functools.partial(_bias_lookup_kernel, span=span),
        grid=(n_tiles, n_heads),
        in_specs=[pl.BlockSpec(memory_space=pltpu.SMEM),
                  pl.BlockSpec(memory_space=pltpu.SMEM),
                  pl.BlockSpec((1, rows, cols), lambda t, h: (t, 0, 0))],
        out_specs=pl.BlockSpec((1, 1, rows, cols), lambda t, h: (t, h, 0, 0)),
        out_shape=jax.ShapeDtypeStruct((n_tiles, n_heads, rows, cols), jnp.float32),
        compiler_params=_params(("arbitrary", "arbitrary")),
        name=name,
    )(jnp.asarray(bases), table.astype(jnp.float32), jnp.asarray(idx))


def _inproj_kernel(x_ref, g_ref, b_ref, w_ref, w_t_ref, *out_refs, apply_ln):
    x = x_ref[...]
    if apply_ln:
        xn_ref, *out_refs = out_refs
        x = _layer_norm_f32(x, g_ref[...], b_ref[...])
        xn_ref[...] = x
    *proj_refs, qc_t_ref, vc_t_ref = out_refs
    xb = x.astype(jnp.bfloat16)
    for split, o_ref in zip(TOKEN_MAJOR_SPLITS, proj_refs):
        off, width = IN_OFFSETS[split], IN_SPLITS[split]
        o_ref[...] = jnp.dot(xb, w_ref[:, off:off + width],
                             preferred_element_type=jnp.float32).astype(o_ref.dtype)
    nt_dims = (((1,), (1,)), ((), ()))
    qv_t = lax.dot_general(w_t_ref[...], xb, nt_dims, preferred_element_type=jnp.float32)
    qc_t_ref[0] = (qv_t[:C_QK_W] * (DIFF_QK_DIM ** -0.5 * LOG2E)).astype(qc_t_ref.dtype)
    vc_t_ref[0] = qv_t[C_QK_W:].astype(vc_t_ref.dtype)


def _inproj(x2d, g, b, w_bf16, seq, apply_ln):
    n_tok = x2d.shape[0]
    bsz = n_tok // seq
    tiles_per_seq = seq // ROW_TILE
    row = lambda i: (i, 0)
    const = lambda i: (0, 0)
    feat_major = lambda i: (i // tiles_per_seq, 0, i % tiles_per_seq)
    out_shape = [jax.ShapeDtypeStruct((n_tok, IN_SPLITS[s]), jnp.bfloat16) for s in TOKEN_MAJOR_SPLITS]
    out_specs = [pl.BlockSpec((ROW_TILE, IN_SPLITS[s]), row) for s in TOKEN_MAJOR_SPLITS]
    for width in (C_QK_W, C_V_W):
        out_shape.append(jax.ShapeDtypeStruct((bsz, width, seq), jnp.bfloat16))
        out_specs.append(pl.BlockSpec((1, width, ROW_TILE), feat_major))
    if apply_ln:
        out_shape = [jax.ShapeDtypeStruct((n_tok, D_MODEL), jnp.float32)] + out_shape
        out_specs = [pl.BlockSpec((ROW_TILE, D_MODEL), row)] + out_specs
    w_t = jnp.concatenate([w_bf16[:, IN_OFFSETS[s]:IN_OFFSETS[s] + IN_SPLITS[s]] for s in (QC_SPLIT, VC_SPLIT)],
                          axis=1).T
    return pl.pallas_call(
        functools.partial(_inproj_kernel, apply_ln=apply_ln),
        grid=(n_tok // ROW_TILE,),
        in_specs=[pl.BlockSpec((ROW_TILE, D_MODEL), row),
                  pl.BlockSpec((1, D_MODEL), const),
                  pl.BlockSpec((1, D_MODEL), const),
                  pl.BlockSpec((D_MODEL, IN_WIDTH), const),
                  pl.BlockSpec((C_QK_W + C_V_W, D_MODEL), const)],
        out_specs=out_specs,
        out_shape=out_shape,
        compiler_params=_params(("parallel",)),
        name="inproj_ln" if apply_ln else "inproj",
    )(x2d, g.reshape(1, D_MODEL), b.reshape(1, D_MODEL), w_bf16, w_t)


def _na_kernel(q_ref, k_ref, v_ref, t2_ref, o_ref, *, rows):
    i = pl.program_id(1)
    r0 = i * NA_ROWS_PER_STEP
    pair_w = 2 * HEAD_DIM
    win_keys = NA_WIN_ROWS * GRID_W
    low = lax.broadcasted_iota(jnp.int32, (GRID_W, pair_w), 1) < HEAD_DIM
    nt_dims = (((1,), (1,)), ((), ()))
    stages = [(ri, hp) for ri in range(NA_ROWS_PER_STEP) for hp in range(NA_HEADS // 2)]

    def scores(ri, hp):
        r = r0 + ri
        row_start = jnp.clip(r - NA_WIN_ROWS // 2, 0, rows - NA_WIN_ROWS)
        kstart = pl.multiple_of(row_start * GRID_W, GRID_W)
        lanes = slice(hp * pair_w, (hp + 1) * pair_w)
        q_pair = q_ref[0, ri * GRID_W:(ri + 1) * GRID_W, lanes].astype(jnp.float32)
        q_both = jnp.concatenate([jnp.where(low, q_pair, 0.0), jnp.where(low, 0.0, q_pair)], axis=0)
        k_pair = k_ref[0, pl.ds(kstart, win_keys), lanes]
        s = lax.dot_general(q_both.astype(jnp.bfloat16), k_pair, nt_dims,
                            preferred_element_type=jnp.float32)
        p0 = row_start - r + NA_WIN_ROWS - 1
        bias = jnp.concatenate(
            [jnp.concatenate([t2_ref[p0 + 2 * kk, 2 * hp + e] for kk in range(NA_WIN_ROWS // 2)], axis=1)
             for e in range(2)], axis=0)
        return s + bias, kstart

    def finish(ri, hp, s, kstart):
        lanes = slice(hp * pair_w, (hp + 1) * pair_w)
        m = jnp.max(s, axis=-1, keepdims=True)
        p = jnp.exp2(s - m)
        l = jnp.sum(p, axis=-1, keepdims=True)
        v_pair = v_ref[0, pl.ds(kstart, win_keys), lanes]
        o = jnp.dot(p.astype(jnp.bfloat16), v_pair, preferred_element_type=jnp.float32) / l
        o_ref[0, ri * GRID_W:(ri + 1) * GRID_W, lanes] = jnp.where(low, o[:GRID_W], o[GRID_W:]).astype(o_ref.dtype)

    pending = [scores(*st) for st in stages[:ATTN_LOOKAHEAD]]
    for t, (ri, hp) in enumerate(stages):
        current = pending.pop(0)
        if t + ATTN_LOOKAHEAD < len(stages):
            pending.append(scores(*stages[t + ATTN_LOOKAHEAD]))
        finish(ri, hp, *current)


def _na_attention(q, k, v, rpb):
    bsz, seq, _ = q.shape
    rows = seq // GRID_W
    n_steps = rows // NA_ROWS_PER_STEP
    nq = NA_ROWS_PER_STEP * GRID_W
    idx = _na_index_tables()
    t2 = _bias_lookup(rpb.reshape(NA_HEADS, -1).astype(jnp.float32) * LOG2E, idx, "na_bias")
    return pl.pallas_call(
        functools.partial(_na_kernel, rows=rows),
        grid=(bsz, n_steps),
        in_specs=[pl.BlockSpec((1, nq, A_W), lambda b, i: (b, i, 0)),
                  pl.BlockSpec((1, seq, A_W), lambda b, i: (b, 0, 0)),
                  pl.BlockSpec((1, seq, A_W), lambda b, i: (b, 0, 0)),
                  pl.BlockSpec(t2.shape, lambda b, i: (0, 0, 0, 0))],
        out_specs=pl.BlockSpec((1, nq, A_W), lambda b, i: (b, i, 0)),
        out_shape=jax.ShapeDtypeStruct((bsz, seq, A_W), jnp.bfloat16),
        compiler_params=_params(("parallel", "arbitrary")),
        name="na_attention",
    )(q, k, v, t2)


def _sw_head_order():
    group = SW_HEADS // SW_KV_HEADS
    assert SW_KV_HEADS == 2
    return [h for g in range(group) for h in (g, g + group)]


def _permute_head_blocks(w, axis, start, order):
    def block(lo, hi):
        return lax.slice_in_dim(w, lo, hi, axis=axis)
    end = start + len(order) * HEAD_DIM
    parts = [block(0, start)] + [block(start + h * HEAD_DIM, start + (h + 1) * HEAD_DIM) for h in order]
    parts.append(block(end, w.shape[axis]))
    return jnp.concatenate([p for p in parts if p.shape[axis] > 0], axis=axis)


def _sw_kernel(sink_ref, q_ref, k_ref, v_ref, bias_ref, o_ref, *, seq):
    i = pl.program_id(1)
    group = SW_HEADS // SW_KV_HEADS
    pair_w = 2 * HEAD_DIM
    low = lax.broadcasted_iota(jnp.int32, (SW_TQ, pair_w), 1) < HEAD_DIM
    nt_dims = (((1,), (1,)), ((), ()))
    stages = [(tt, g, e) for tt in range(SW_TILES_PER_STEP) for g in range(group) for e in range(SW_KV_HEADS)]

    def window(tt):
        tile = i * SW_TILES_PER_STEP + tt
        kstart = pl.multiple_of(jnp.clip(tile * SW_TQ - SW_WINDOW, 0, seq - SW_TK), SW_WINDOW)
        bias_col = pl.multiple_of(kstart - tile * SW_TQ + 2 * SW_WINDOW, SW_WINDOW)
        return kstart, bias_col

    def scores(tt, g, e):
        kstart, bias_col = window(tt)
        q_grp = q_ref[0, tt * SW_TQ:(tt + 1) * SW_TQ, g * pair_w:(g + 1) * pair_w].astype(jnp.float32)
        q_head = jnp.where(low, q_grp, 0.0) if e == 0 else jnp.where(low, 0.0, q_grp)
        s = lax.dot_general(q_head.astype(jnp.bfloat16), k_ref[0, pl.ds(kstart, SW_TK), :], nt_dims,
                            preferred_element_type=jnp.float32)
        return s + bias_ref[0, g + e * group, :, pl.ds(bias_col, SW_TK)]

    def attend(tt, g, e, s):
        kstart, _ = window(tt)
        sink = sink_ref[g + e * group] * LOG2E
        m = jnp.maximum(jnp.max(s, axis=-1, keepdims=True), sink)
        p = jnp.exp2(s - m)
        l = jnp.sum(p, axis=-1, keepdims=True) + jnp.exp2(sink - m)
        return jnp.dot(p.astype(jnp.bfloat16), v_ref[0, pl.ds(kstart, SW_TK), :],
                       preferred_element_type=jnp.float32) / l

    pending = [scores(*st) for st in stages[:ATTN_LOOKAHEAD]]
    o_low = None
    for t, (tt, g, e) in enumerate(stages):
        current = pending.pop(0)
        if t + ATTN_LOOKAHEAD < len(stages):
            pending.append(scores(*stages[t + ATTN_LOOKAHEAD]))
        o = attend(tt, g, e, current)
        if e == 0:
            o_low = o
        else:
            o_ref[0, tt * SW_TQ:(tt + 1) * SW_TQ, g * pair_w:(g + 1) * pair_w] = (
                jnp.where(low, o_low, o).astype(o_ref.dtype))


def _sw_attention(q, k, v, sink, sw_table):
    bsz, seq, _ = q.shape
    step_q = SW_TILES_PER_STEP * SW_TQ
    assert seq % step_q == 0
    bias = _bias_lookup(sw_table.T.astype(jnp.float32) * LOG2E, _sw_index_tables(), "sw_bias")
    return pl.pallas_call(
        functools.partial(_sw_kernel, seq=seq),
        grid=(bsz, seq // step_q),
        in_specs=[pl.BlockSpec(memory_space=pltpu.SMEM),
                  pl.BlockSpec((1, step_q, B_Q_W), lambda b, i: (b, i, 0)),
                  pl.BlockSpec((1, seq, B_KV_W), lambda b, i: (b, 0, 0)),
                  pl.BlockSpec((1, seq, B_KV_W), lambda b, i: (b, 0, 0)),
                  pl.BlockSpec(bias.shape, lambda b, i: (0, 0, 0, 0), pipeline_mode=pl.Buffered(1))],
        out_specs=pl.BlockSpec((1, step_q, B_Q_W), lambda b, i: (b, i, 0)),
        out_shape=jax.ShapeDtypeStruct((bsz, seq, B_Q_W), jnp.bfloat16),
        compiler_params=_params(("parallel", "arbitrary")),
        name="sw_attention",
    )(sink.astype(jnp.float32), q, k, v, bias)


def _df_kernel(cfar_ref, q_t_ref, k_ref, v_t_ref, bias_ref, lq_ref, lk_ref, g_ref, o_ref,
               qm_scr, s_scr, mx_scr, m_scr, acc_scr, *, lam_init, n_chunks):
    i = pl.program_id(1)
    n_stat = 2 * DIFF_HEADS
    n_near = 2 * DF_NEAR + 2
    n_far = n_chunks - n_near
    per_group = DF_QM_ROWS // DIFF_QK_DIM

    q_t = q_t_ref[0]
    zeros = jnp.zeros((DF_QM_ROWS, DF_TQ), jnp.bfloat16)
    for n in range(n_stat):
        qm_scr[n] = zeros
        dst = (n % per_group) * DIFF_QK_DIM
        qm_scr[n, dst:dst + DIFF_QK_DIM, :] = q_t[n * DIFF_QK_DIM:(n + 1) * DIFF_QK_DIM, :]
    m_scr[...] = jnp.full_like(m_scr, NEG)
    acc_scr[...] = jnp.zeros_like(acc_scr)
    ones = jnp.ones((DF_ONES_ROWS, DF_TK), jnp.bfloat16)

    near0 = jnp.clip(i - DF_NEAR, 0, n_far)

    def near_chunk(u):
        return near0 + u

    def far_chunk(t):
        return t + jnp.where(t >= near0, n_near, 0)

    def issue_scores(j, n, near):
        kstart = pl.multiple_of(j * DF_TK, DF_TK)
        g = n // per_group
        k_grp = k_ref[0, pl.ds(kstart, DF_TK), g * DF_QM_ROWS:(g + 1) * DF_QM_ROWS]
        s = jnp.dot(k_grp, qm_scr[n], preferred_element_type=jnp.float32)
        if near:
            tile = jnp.clip(j - i, -DF_NEAR - 1, DF_NEAR + 1) + DF_NEAR + 1
            s = s + bias_ref[tile, n // 2]
        s_scr[n] = s
        mx_scr[n] = jnp.max(s, axis=0, keepdims=True)

    def consume(j, n, cb):
        kstart = pl.multiple_of(j * DF_TK, DF_TK)
        h = n // 2
        v_aug = jnp.concatenate(
            [v_t_ref[0, h * DIFF_V_DIM:(h + 1) * DIFF_V_DIM, pl.ds(kstart, DF_TK)], ones], axis=0)
        m_prev = m_scr[n]
        m_new = jnp.maximum(m_prev, mx_scr[n] + cb)
        alpha = jnp.exp2(m_prev - m_new)
        p = jnp.exp2(s_scr[n] - (m_new - cb)).astype(jnp.bfloat16)
        acc_scr[n] = alpha * acc_scr[n] + jnp.dot(v_aug, p, preferred_element_type=jnp.float32)
        m_scr[n] = m_new

    def run_chunk(j, j_next, near, next_near, cbs):
        for n in range(n_stat):
            ahead = n + DF_LOOKAHEAD
            if ahead < n_stat:
                issue_scores(j, ahead, near)
            else:
                issue_scores(j_next, ahead - n_stat, next_near)
            consume(j, n, cbs[n // 2])

    for n in range(DF_LOOKAHEAD):
        issue_scores(near_chunk(0), n, True)
    for u in range(n_near):
        last = u == n_near - 1
        run_chunk(near_chunk(u), far_chunk(0) if last else near_chunk(u + 1), True, not last,
                  [0.0] * DIFF_HEADS)

    def far_group(it, carry):
        for u in range(DF_FAR_UNROLL):
            t = DF_FAR_UNROLL * it + u
            j = far_chunk(t)
            j_next = far_chunk(jnp.minimum(t + 1, n_far - 1))
            side = (j > i).astype(jnp.int32)
            run_chunk(j, j_next, False, False, [cfar_ref[side, h] for h in range(DIFF_HEADS)])
        return carry

    lax.fori_loop(0, n_far // DF_FAR_UNROLL, far_group, 0)

    dots = jnp.sum(lq_ref[...] * lk_ref[...], axis=-1, keepdims=True)
    lam = jnp.exp(dots[0:1]) - jnp.exp(dots[1:2]) + lam_init
    outs = []
    for h in range(DIFF_HEADS):
        a1 = acc_scr[2 * h]
        a2 = acc_scr[2 * h + 1]
        o1 = a1[:DIFF_V_DIM] / a1[DIFF_V_DIM:DIFF_V_DIM + 1]
        o2 = a2[:DIFF_V_DIM] / a2[DIFF_V_DIM:DIFF_V_DIM + 1]
        of = o1 - lam * o2
        of = of * lax.rsqrt(jnp.mean(of * of, axis=0, keepdims=True) + LN_EPS)
        outs.append(of * g_ref[...] * (1.0 - lam_init))
    o_ref[0] = jnp.concatenate(outs, axis=0).T.astype(o_ref.dtype)


def _df_attention(q_t, k, v_t, lam_q, lam_k, subln_g, diff_table, lam_init):
    bsz, seq, _ = k.shape
    n_chunks = seq // DF_TK
    assert DF_TQ == DF_TK and DF_LOOKAHEAD < 2 * DIFF_HEADS
    assert (n_chunks - 2 * DF_NEAR - 2) % DF_FAR_UNROLL == 0 and n_chunks > 2 * DF_NEAR + 2
    table_t = diff_table.astype(jnp.float32).T * LOG2E
    bias = _bias_lookup(table_t, _df_bucket_tiles(), "diff_bias")
    cfar = jnp.stack([table_t[:, N_BUCKETS // 2 - 1], table_t[:, N_BUCKETS - 1]])
    n_stat = 2 * DIFF_HEADS
    n_tiles = 2 * DF_NEAR + 3
    const2 = lambda b, i: (0, 0)
    return pl.pallas_call(
        functools.partial(_df_kernel, lam_init=lam_init, n_chunks=n_chunks),
        grid=(bsz, seq // DF_TQ),
        in_specs=[pl.BlockSpec(memory_space=pltpu.SMEM),
                  pl.BlockSpec((1, C_QK_W, DF_TQ), lambda b, i: (b, 0, i)),
                  pl.BlockSpec((1, seq, C_QK_W), lambda b, i: (b, 0, 0)),
                  pl.BlockSpec((1, C_V_W, seq), lambda b, i: (b, 0, 0)),
                  pl.BlockSpec((n_tiles, DIFF_HEADS, DF_TK, DF_TQ), lambda b, i: (0, 0, 0, 0),
                               pipeline_mode=pl.Buffered(1)),
                  pl.BlockSpec((2, DIFF_QK_DIM), const2),
                  pl.BlockSpec((2, DIFF_QK_DIM), const2),
                  pl.BlockSpec((DIFF_V_DIM, 1), const2)],
        out_specs=pl.BlockSpec((1, DF_TQ, C_V_W), lambda b, i: (b, i, 0)),
        out_shape=jax.ShapeDtypeStruct((bsz, seq, C_V_W), jnp.bfloat16),
        scratch_shapes=[pltpu.VMEM((n_stat, DF_QM_ROWS, DF_TQ), jnp.bfloat16),
                        pltpu.VMEM((n_stat, DF_TK, DF_TQ), jnp.float32),
                        pltpu.VMEM((n_stat, 1, DF_TQ), jnp.float32),
                        pltpu.VMEM((n_stat, 1, DF_TQ), jnp.float32),
                        pltpu.VMEM((n_stat, DF_ACC_ROWS, DF_TQ), jnp.float32)],
        compiler_params=_params(("parallel", "arbitrary")),
        name="diff_attention",
    )(cfar, q_t, k, v_t, bias, lam_q.astype(jnp.float32), lam_k.astype(jnp.float32),
      subln_g.astype(jnp.float32).reshape(DIFF_V_DIM, 1))


FF_CHUNK = 1024
MIX_SUBTILE = 256


def _mix_ffn_kernel(x_ref, oa_ref, ob_ref, oc_ref, wo_ref, g1_ref, b1_ref, w1_ref, w2_ref, g2_ref, b2_ref,
                    o_ref, *, alpha):
    subtiles = [slice(r, r + MIX_SUBTILE) for r in range(0, ROW_TILE, MIX_SUBTILE)]

    def mix(rows):
        acc = jnp.dot(oa_ref[rows, :], wo_ref[0:A_W, :], preferred_element_type=jnp.float32)
        acc += jnp.dot(ob_ref[rows, :], wo_ref[A_W:A_W + B_Q_W, :], preferred_element_type=jnp.float32)
        acc += jnp.dot(oc_ref[rows, :], wo_ref[A_W + B_Q_W:MIX_WIDTH, :], preferred_element_type=jnp.float32)
        return alpha * x_ref[rows, :] + acc

    mixed = [mix(rows) for rows in subtiles]
    xs = [_layer_norm_f32(v, g1_ref[...], b1_ref[...]) for v in mixed]
    for rows, x in zip(subtiles, xs):
        xb = x.astype(jnp.bfloat16)
        y = jnp.zeros((MIX_SUBTILE, D_MODEL), jnp.float32)
        for c in range(D_FF // FF_CHUNK):
            sl = slice(c * FF_CHUNK, (c + 1) * FF_CHUNK)
            h = jnp.maximum(jnp.dot(xb, w1_ref[:, sl], preferred_element_type=jnp.float32), 0.0)
            y += jnp.dot((h * h).astype(jnp.bfloat16), w2_ref[sl, :], preferred_element_type=jnp.float32)
        o_ref[rows, :] = _layer_norm_f32(alpha * x + y, g2_ref[...], b2_ref[...])


def _mix_ffn(x2d, oa, ob, oc, wo_bf16, g1, b1, w1_bf16, w2_bf16, g2, b2, alpha):
    n_tok = x2d.shape[0]
    row = lambda i: (i, 0)
    const = lambda i: (0, 0)
    vec = pl.BlockSpec((1, D_MODEL), const)
    resident = functools.partial(pl.BlockSpec, index_map=const, pipeline_mode=pl.Buffered(1))
    return pl.pallas_call(
        functools.partial(_mix_ffn_kernel, alpha=alpha),
        grid=(n_tok // ROW_TILE,),
        in_specs=[pl.BlockSpec((ROW_TILE, D_MODEL), row),
                  pl.BlockSpec((ROW_TILE, A_W), row),
                  pl.BlockSpec((ROW_TILE, B_Q_W), row),
                  pl.BlockSpec((ROW_TILE, C_V_W), row),
                  resident((MIX_WIDTH, D_MODEL)), vec, vec,
                  resident((D_MODEL, D_FF)), resident((D_FF, D_MODEL)), vec, vec],
        out_specs=pl.BlockSpec((ROW_TILE, D_MODEL), row),
        out_shape=jax.ShapeDtypeStruct((n_tok, D_MODEL), jnp.float32),
        compiler_params=_params(("parallel",)),
        name="mix_ffn",
    )(x2d, oa, ob, oc, wo_bf16, g1.reshape(1, D_MODEL), b1.reshape(1, D_MODEL),
      w1_bf16, w2_bf16, g2.reshape(1, D_MODEL), b2.reshape(1, D_MODEL))


def kernel(x, ln_in_g, ln_in_b, t5_table, w_in, w_out, na_rpb, sw_sink, diff_lam_q, diff_lam_k, diff_subln_g,
           ln_mix_g, ln_mix_b, w_ff1, w_ff2, ln_ff_g, ln_ff_b):
    alpha = (2 * DEPTH) ** 0.25
    bsz, seq, _ = x.shape
    assert x.shape[-1] == D_MODEL and seq % max(ROW_TILE, SW_TQ, DF_TQ, DF_TK) == 0
    assert (seq // GRID_W) % NA_ROWS_PER_STEP == 0 and seq // GRID_W >= NA_WIN_ROWS
    sw_table = t5_table[:, :SW_HEADS]
    diff_table = t5_table[:, SW_HEADS:]
    col_scale = np.ones((IN_WIDTH,), np.float32)
    for split in (0, 3):
        col_scale[IN_OFFSETS[split]:IN_OFFSETS[split] + IN_SPLITS[split]] = HEAD_DIM ** -0.5 * LOG2E
    x2d = x.reshape(bsz * seq, D_MODEL)
    for l in range(DEPTH):
        lam_init = 0.8 - 0.6 * math.exp(-0.3 * l)
        w_in_l = _permute_head_blocks((w_in[l] * col_scale).astype(jnp.bfloat16), 1, IN_OFFSETS[3], _sw_head_order())
        w_out_l = _permute_head_blocks(w_out[l].astype(jnp.bfloat16), 0, A_W, _sw_head_order())
        if l == 0:
            x2d, *proj = _inproj(x2d, ln_in_g, ln_in_b, w_in_l, seq, True)
        else:
            proj = _inproj(x2d, ln_in_g, ln_in_b, w_in_l, seq, False)
        *tok_major, qc_t, vc_t = proj
        qa, ka, va, qb, kb, vb, kc = [p.reshape(bsz, seq, -1) for p in tok_major]
        oa = _na_attention(qa, ka, va, na_rpb[l])
        ob = _sw_attention(qb, kb, vb, sw_sink[l], sw_table)
        oc = _df_attention(qc_t, kc, vc_t, diff_lam_q[l], diff_lam_k[l], diff_subln_g[l], diff_table, lam_init)
        x2d = _mix_ffn(x2d, oa.reshape(-1, A_W), ob.reshape(-1, B_Q_W), oc.reshape(-1, C_V_W),
                       w_out_l, ln_mix_g[l], ln_mix_b[l], w_ff1[l].astype(jnp.bfloat16),
                       w_ff2[l].astype(jnp.bfloat16), ln_ff_g[l], ln_ff_b[l], alpha)
    return x2d.reshape(bsz, seq, D_MODEL)
```

```python
import functools
import math

import numpy as np
import jax
import jax.numpy as jnp
from jax import lax
from jax.experimental import pallas as pl
from jax.experimental.pallas import tpu as pltpu

D_MODEL = 1024
DEPTH = 2
GRID_W = 64
HEAD_DIM = 64
NA_HEADS = 4
NA_WIN_ROWS = 8
NA_WIN_COLS = 16
SW_HEADS = 8
SW_KV_HEADS = 2
SW_WINDOW = 128
DIFF_HEADS = 4
DIFF_QK_DIM = HEAD_DIM // 2
DIFF_V_DIM = HEAD_DIM
D_FF = 4 * D_MODEL
N_BUCKETS = 32
MAX_DISTANCE = 128
LN_EPS = 1e-5
NEG = -1e30
LOG2E = math.log2(math.e)

A_W = NA_HEADS * HEAD_DIM
B_Q_W = SW_HEADS * HEAD_DIM
B_KV_W = SW_KV_HEADS * HEAD_DIM
C_QK_W = DIFF_HEADS * 2 * DIFF_QK_DIM
C_V_W = DIFF_HEADS * DIFF_V_DIM
IN_SPLITS = (A_W, A_W, A_W, B_Q_W, B_KV_W, B_KV_W, C_QK_W, C_QK_W, C_V_W)
IN_OFFSETS = tuple(int(v) for v in np.cumsum((0,) + IN_SPLITS[:-1]))
IN_WIDTH = sum(IN_SPLITS)
MIX_WIDTH = A_W + B_Q_W + C_V_W
TOKEN_MAJOR_SPLITS = (0, 1, 2, 3, 4, 5, 7)
QC_SPLIT, VC_SPLIT = 6, 8

VMEM_LIMIT_BYTES = 56 * 1024 * 1024

ROW_TILE = 1024
NA_ROWS_PER_STEP = 32
NA_LOOKAHEAD = 3
SW_LOOKAHEAD = 2
SW_TQ = 256
SW_TK = SW_TQ + 2 * SW_WINDOW
SW_TILES_PER_STEP = 4
DF_TQ = 256
DF_TK = 256
DF_ONES_ROWS = 16
DF_ACC_ROWS = DIFF_V_DIM + DF_ONES_ROWS
DF_NEAR = MAX_DISTANCE // DF_TK + 1
DF_LOOKAHEAD = 5
DF_QM_ROWS = 128
DF_FAR_UNROLL = 14
BIAS_STRIP_ELEMS = 8 * 1024


def _params(semantics):
    return pltpu.CompilerParams(dimension_semantics=semantics, vmem_limit_bytes=VMEM_LIMIT_BYTES)


def _layer_norm_f32(x, g, b):
    mu = jnp.mean(x, axis=-1, keepdims=True)
    xc = x - mu
    var = jnp.mean(xc * xc, axis=-1, keepdims=True)
    return xc * lax.rsqrt(var + LN_EPS) * g + b


def _t5_bucket_np(rel):
    rel = np.asarray(rel, dtype=np.int64)
    nb = N_BUCKETS // 2
    max_exact = nb // 2
    n = np.abs(rel)
    nn = np.maximum(n, 1)
    floor_log2_sq = np.floor(np.log2((nn * nn).astype(np.float64)) + 1e-9).astype(np.int64)
    large = np.minimum(max_exact + floor_log2_sq - 6, nb - 1)
    return (np.where(rel > 0, nb, 0) + np.where(n < max_exact, n, large)).astype(np.int32)


def _na_index_tables():
    n_dr = 2 * NA_WIN_ROWS - 1
    n_dc = 2 * NA_WIN_COLS - 1
    p = np.arange(n_dr - 1)[:, None, None]
    c = np.arange(GRID_W)[None, :, None]
    lane = np.arange(2 * GRID_W)[None, None, :]
    kc = lane % GRID_W
    col_start = np.clip(c - NA_WIN_COLS // 2, 0, GRID_W - NA_WIN_COLS)
    valid = (kc >= col_start) & (kc < col_start + NA_WIN_COLS)
    dc = np.clip(kc - c, -(NA_WIN_COLS - 1), NA_WIN_COLS - 1) + NA_WIN_COLS - 1
    idx = (p + lane // GRID_W) * n_dc + dc
    return np.where(valid, idx, -1).astype(np.int32)


def _sw_index_tables(seq):
    n_steps = seq // SW_TQ
    buckets = []
    for step in (0, 1, n_steps - 1):
        qbase = step * SW_TQ
        kstart = int(np.clip(qbase - SW_WINDOW, 0, seq - SW_TK))
        rel = (kstart + np.arange(SW_TK)[None, :]) - (qbase + np.arange(SW_TQ)[:, None])
        buckets.append(np.where(np.abs(rel) <= SW_WINDOW, _t5_bucket_np(rel), -1))
    return np.stack(buckets).astype(np.int32)


def _df_bucket_tiles():
    tiles = []
    for d in range(-DF_NEAR - 1, DF_NEAR + 2):
        rel = d * DF_TK + np.arange(DF_TK)[:, None] - np.arange(DF_TQ)[None, :]
        tiles.append(_t5_bucket_np(rel))
    tiles = np.stack(tiles)
    assert (tiles[0] == tiles[0, 0, 0]).all() and (tiles[-1] == tiles[-1, 0, 0]).all()
    return tiles


def _bias_lookup_kernel(base_ref, table_ref, idx_ref, o_ref, *, span):
    base = base_ref[pl.program_id(0)]
    h = pl.program_id(1)
    _, rows, cols = idx_ref.shape
    strip_rows = min(rows, max(8, BIAS_STRIP_ELEMS // cols))
    assert rows % strip_rows == 0

    def strip(r, carry):
        r0 = pl.multiple_of(r * strip_rows, strip_rows)
        idx = idx_ref[0, pl.ds(r0, strip_rows), :]
        acc = jnp.full(idx.shape, NEG, jnp.float32)
        for e in range(span):
            acc = jnp.where(idx == base + e, table_ref[h, base + e], acc)
        o_ref[0, 0, pl.ds(r0, strip_rows), :] = acc
        return carry

    lax.fori_loop(0, rows // strip_rows, strip, 0)


def _bias_lookup(table, idx, name):
    n_heads, n_entries = table.shape
    n_tiles, rows, cols = idx.shape
    flat = idx.reshape(n_tiles, -1)
    lo = np.where(flat >= 0, flat, n_entries).min(axis=1)
    span = int((flat.max(axis=1) - lo).max()) + 1
    bases = np.minimum(lo, n_entries - span).astype(np.int32)
    return pl.pallas_call(
        functools.partial(_bias_lookup_kernel, span=span),
        grid=(n_tiles, n_heads),
        in_specs=[pl.BlockSpec(memory_space=pltpu.SMEM),
                  pl.BlockSpec(memory_space=pltpu.SMEM),
                  pl.BlockSpec((1, rows, cols), lambda t, h: (t, 0, 0))],
        out_specs=pl.BlockSpec((1, 1, rows, cols), lambda t, h: (t, h, 0, 0)),
        out_shape=jax.ShapeDtypeStruct((n_tiles, n_heads, rows, cols), jnp.float32),
        compiler_params=_params(("arbitrary", "arbitrary")),
        name=name,
    )(jnp.asarray(bases), table.astype(jnp.float32), jnp.asarray(idx))


def _inproj_kernel(x_ref, g_ref, b_ref, w_ref, wq_t_ref, wv_t_ref, *out_refs, apply_ln):
    x = x_ref[...]
    if apply_ln:
        xn_ref, *out_refs = out_refs
        x = _layer_norm_f32(x, g_ref[...], b_ref[...])
        xn_ref[...] = x
    *proj_refs, qc_t_ref, vc_t_ref = out_refs
    xb = x.astype(jnp.bfloat16)
    for split, o_ref in zip(TOKEN_MAJOR_SPLITS, proj_refs):
        off, width = IN_OFFSETS[split], IN_SPLITS[split]
        o_ref[...] = jnp.dot(xb, w_ref[:, off:off + width],
                             preferred_element_type=jnp.float32).astype(o_ref.dtype)
    nt_dims = (((1,), (1,)), ((), ()))
    qc_t = lax.dot_general(wq_t_ref[...], xb, nt_dims, preferred_element_type=jnp.float32)
    qc_t_ref[0] = (qc_t * (DIFF_QK_DIM ** -0.5 * LOG2E)).astype(qc_t_ref.dtype)
    vc_t = lax.dot_general(wv_t_ref[...], xb, nt_dims, preferred_element_type=jnp.float32)
    vc_t_ref[0] = vc_t.astype(vc_t_ref.dtype)


def _inproj(x2d, g, b, w_bf16, seq, apply_ln):
    n_tok = x2d.shape[0]
    bsz = n_tok // seq
    tiles_per_seq = seq // ROW_TILE
    row = lambda i: (i, 0)
    const = lambda i: (0, 0)
    feat_major = lambda i: (i // tiles_per_seq, 0, i % tiles_per_seq)
    out_shape = [jax.ShapeDtypeStruct((n_tok, IN_SPLITS[s]), jnp.bfloat16) for s in TOKEN_MAJOR_SPLITS]
    out_specs = [pl.BlockSpec((ROW_TILE, IN_SPLITS[s]), row) for s in TOKEN_MAJOR_SPLITS]
    for width in (C_QK_W, C_V_W):
        out_shape.append(jax.ShapeDtypeStruct((bsz, width, seq), jnp.bfloat16))
        out_specs.append(pl.BlockSpec((1, width, ROW_TILE), feat_major))
    if apply_ln:
        out_shape = [jax.ShapeDtypeStruct((n_tok, D_MODEL), jnp.float32)] + out_shape
        out_specs = [pl.BlockSpec((ROW_TILE, D_MODEL), row)] + out_specs
    wq_t = w_bf16[:, IN_OFFSETS[QC_SPLIT]:IN_OFFSETS[QC_SPLIT] + C_QK_W].T
    wv_t = w_bf16[:, IN_OFFSETS[VC_SPLIT]:IN_OFFSETS[VC_SPLIT] + C_V_W].T
    return pl.pallas_call(
        functools.partial(_inproj_kernel, apply_ln=apply_ln),
        grid=(n_tok // ROW_TILE,),
        in_specs=[pl.BlockSpec((ROW_TILE, D_MODEL), row),
                  pl.BlockSpec((1, D_MODEL), const),
                  pl.BlockSpec((1, D_MODEL), const),
                  pl.BlockSpec((D_MODEL, IN_WIDTH), const),
                  pl.BlockSpec((C_QK_W, D_MODEL), const),
                  pl.BlockSpec((C_V_W, D_MODEL), const)],
        out_specs=out_specs,
        out_shape=out_shape,
        compiler_params=_params(("parallel",)),
        name="inproj_ln" if apply_ln else "inproj",
    )(x2d, g.reshape(1, D_MODEL), b.reshape(1, D_MODEL), w_bf16, wq_t, wv_t)


def _na_kernel(q_ref, k_ref, v_ref, t2_ref, o_ref, *, rows):
    i = pl.program_id(1)
    r0 = i * NA_ROWS_PER_STEP
    pair_w = 2 * HEAD_DIM
    win_keys = NA_WIN_ROWS * GRID_W
    low = lax.broadcasted_iota(jnp.int32, (GRID_W, pair_w), 1) < HEAD_DIM
    nt_dims = (((1,), (1,)), ((), ()))
    stages = [(ri, hp) for ri in range(NA_ROWS_PER_STEP) for hp in range(NA_HEADS // 2)]

    def scores(ri, hp):
        r = r0 + ri
        row_start = jnp.clip(r - NA_WIN_ROWS // 2, 0, rows - NA_WIN_ROWS)
        kstart = pl.multiple_of(row_start * GRID_W, GRID_W)
        lanes = slice(hp * pair_w, (hp + 1) * pair_w)
        q_pair = q_ref[0, ri * GRID_W:(ri + 1) * GRID_W, lanes].astype(jnp.float32)
        q_both = jnp.concatenate([jnp.where(low, q_pair, 0.0), jnp.where(low, 0.0, q_pair)], axis=0)
        k_pair = k_ref[0, pl.ds(kstart, win_keys), lanes]
        s = lax.dot_general(q_both.astype(jnp.bfloat16), k_pair, nt_dims,
                            preferred_element_type=jnp.float32)
        p0 = row_start - r + NA_WIN_ROWS - 1
        bias = jnp.concatenate(
            [jnp.concatenate([t2_ref[p0 + 2 * kk, 2 * hp + e] for kk in range(NA_WIN_ROWS // 2)], axis=1)
             for e in range(2)], axis=0)
        return s + bias, kstart

    def finish(ri, hp, s, kstart):
        lanes = slice(hp * pair_w, (hp + 1) * pair_w)
        m = jnp.max(s, axis=-1, keepdims=True)
        p = jnp.exp2(s - m)
        l = jnp.sum(p, axis=-1, keepdims=True)
        v_pair = v_ref[0, pl.ds(kstart, win_keys), lanes]
        o = jnp.dot(p.astype(jnp.bfloat16), v_pair, preferred_element_type=jnp.float32) / l
        o_ref[0, ri * GRID_W:(ri + 1) * GRID_W, lanes] = jnp.where(low, o[:GRID_W], o[GRID_W:]).astype(o_ref.dtype)

    pending = [scores(*st) for st in stages[:NA_LOOKAHEAD]]
    for t, (ri, hp) in enumerate(stages):
        current = pending.pop(0)
        if t + NA_LOOKAHEAD < len(stages):
            pending.append(scores(*stages[t + NA_LOOKAHEAD]))
        finish(ri, hp, *current)


def _na_attention(q, k, v, rpb):
    bsz, seq, _ = q.shape
    rows = seq // GRID_W
    n_steps = rows // NA_ROWS_PER_STEP
    nq = NA_ROWS_PER_STEP * GRID_W
    idx = _na_index_tables()
    t2 = _bias_lookup(rpb.reshape(NA_HEADS, -1).astype(jnp.float32) * LOG2E, idx, "na_bias")
    return pl.pallas_call(
        functools.partial(_na_kernel, rows=rows),
        grid=(bsz, n_steps),
        in_specs=[pl.BlockSpec((1, nq, A_W), lambda b, i: (b, i, 0)),
                  pl.BlockSpec((1, seq, A_W), lambda b, i: (b, 0, 0)),
                  pl.BlockSpec((1, seq, A_W), lambda b, i: (b, 0, 0)),
                  pl.BlockSpec(t2.shape, lambda b, i: (0, 0, 0, 0))],
        out_specs=pl.BlockSpec((1, nq, A_W), lambda b, i: (b, i, 0)),
        out_shape=jax.ShapeDtypeStruct((bsz, seq, A_W), jnp.bfloat16),
        compiler_params=_params(("parallel", "arbitrary")),
        name="na_attention",
    )(q, k, v, t2)


def _sw_head_order():
    group = SW_HEADS // SW_KV_HEADS
    assert SW_KV_HEADS == 2
    return [h for g in range(group) for h in (g, g + group)]


def _permute_head_blocks(w, axis, start, order):
    def block(lo, hi):
        return lax.slice_in_dim(w, lo, hi, axis=axis)
    end = start + len(order) * HEAD_DIM
    parts = [block(0, start)] + [block(start + h * HEAD_DIM, start + (h + 1) * HEAD_DIM) for h in order]
    parts.append(block(end, w.shape[axis]))
    return jnp.concatenate([p for p in parts if p.shape[axis] > 0], axis=axis)


def _sw_kernel(sink_ref, q_ref, k_ref, v_ref, bias_ref, o_ref, *, seq):
    i = pl.program_id(1)
    n_tiles = seq // SW_TQ
    group = SW_HEADS // SW_KV_HEADS
    pair_w = 2 * HEAD_DIM
    low = lax.broadcasted_iota(jnp.int32, (SW_TQ, pair_w), 1) < HEAD_DIM
    nt_dims = (((1,), (1,)), ((), ()))
    stages = [(tt, g, e) for tt in range(SW_TILES_PER_STEP) for g in range(group) for e in range(SW_KV_HEADS)]

    def window(tt):
        tile = i * SW_TILES_PER_STEP + tt
        kstart = pl.multiple_of(jnp.clip(tile * SW_TQ - SW_WINDOW, 0, seq - SW_TK), SW_WINDOW)
        variant = jnp.where(tile == 0, 0, jnp.where(tile == n_tiles - 1, 2, 1))
        return kstart, variant

    def scores(tt, g, e):
        kstart, variant = window(tt)
        q_grp = q_ref[0, tt * SW_TQ:(tt + 1) * SW_TQ, g * pair_w:(g + 1) * pair_w].astype(jnp.float32)
        q_head = jnp.where(low, q_grp, 0.0) if e == 0 else jnp.where(low, 0.0, q_grp)
        s = lax.dot_general(q_head.astype(jnp.bfloat16), k_ref[0, pl.ds(kstart, SW_TK), :], nt_dims,
                            preferred_element_type=jnp.float32)
        return s + bias_ref[variant, g + e * group]

    def attend(tt, g, e, s):
        kstart, _ = window(tt)
        sink = sink_ref[g + e * group] * LOG2E
        m = jnp.maximum(jnp.max(s, axis=-1, keepdims=True), sink)
        p = jnp.exp2(s - m)
        l = jnp.sum(p, axis=-1, keepdims=True) + jnp.exp2(sink - m)
        return jnp.dot(p.astype(jnp.bfloat16), v_ref[0, pl.ds(kstart, SW_TK), :],
                       preferred_element_type=jnp.float32) / l

    pending = [scores(*st) for st in stages[:SW_LOOKAHEAD]]
    o_low = None
    for t, (tt, g, e) in enumerate(stages):
        current = pending.pop(0)
        if t + SW_LOOKAHEAD < len(stages):
            pending.append(scores(*stages[t + SW_LOOKAHEAD]))
        o = attend(tt, g, e, current)
        if e == 0:
            o_low = o
        else:
            o_ref[0, tt * SW_TQ:(tt + 1) * SW_TQ, g * pair_w:(g + 1) * pair_w] = (
                jnp.where(low, o_low, o).astype(o_ref.dtype))


def _sw_attention(q, k, v, sink, sw_table):
    bsz, seq, _ = q.shape
    step_q = SW_TILES_PER_STEP * SW_TQ
    assert seq % step_q == 0
    bias = _bias_lookup(sw_table.T.astype(jnp.float32) * LOG2E, _sw_index_tables(seq), "sw_bias")
    return pl.pallas_call(
        functools.partial(_sw_kernel, seq=seq),
        grid=(bsz, seq // step_q),
        in_specs=[pl.BlockSpec(memory_space=pltpu.SMEM),
                  pl.BlockSpec((1, step_q, B_Q_W), lambda b, i: (b, i, 0)),
                  pl.BlockSpec((1, seq, B_KV_W), lambda b, i: (b, 0, 0)),
                  pl.BlockSpec((1, seq, B_KV_W), lambda b, i: (b, 0, 0)),
                  pl.BlockSpec(bias.shape, lambda b, i: (0, 0, 0, 0), pipeline_mode=pl.Buffered(1))],
        out_specs=pl.BlockSpec((1, step_q, B_Q_W), lambda b, i: (b, i, 0)),
        out_shape=jax.ShapeDtypeStruct((bsz, seq, B_Q_W), jnp.bfloat16),
        compiler_params=_params(("parallel", "arbitrary")),
        name="sw_attention",
    )(sink.astype(jnp.float32), q, k, v, bias)


def _df_kernel(cfar_ref, q_t_ref, k_ref, v_t_ref, bias_ref, lq_ref, lk_ref, g_ref, o_ref,
               qm_scr, s_scr, mx_scr, m_scr, acc_scr, *, lam_init, n_chunks):
    i = pl.program_id(1)
    n_stat = 2 * DIFF_HEADS
    n_near = 2 * DF_NEAR + 2
    n_far = n_chunks - n_near
    per_group = DF_QM_ROWS // DIFF_QK_DIM

    q_t = q_t_ref[0]
    zeros = jnp.zeros((DF_QM_ROWS, DF_TQ), jnp.bfloat16)
    for n in range(n_stat):
        qm_scr[n] = zeros
        dst = (n % per_group) * DIFF_QK_DIM
        qm_scr[n, dst:dst + DIFF_QK_DIM, :] = q_t[n * DIFF_QK_DIM:(n + 1) * DIFF_QK_DIM, :]
    m_scr[...] = jnp.full_like(m_scr, NEG)
    acc_scr[...] = jnp.zeros_like(acc_scr)
    ones = jnp.ones((DF_ONES_ROWS, DF_TK), jnp.bfloat16)

    near0 = jnp.clip(i - DF_NEAR, 0, n_far)

    def near_chunk(u):
        return near0 + u

    def far_chunk(t):
        return t + jnp.where(t >= near0, n_near, 0)

    def issue_scores(j, n, near):
        kstart = pl.multiple_of(j * DF_TK, DF_TK)
        g = n // per_group
        k_grp = k_ref[0, pl.ds(kstart, DF_TK), g * DF_QM_ROWS:(g + 1) * DF_QM_ROWS]
        s = jnp.dot(k_grp, qm_scr[n], preferred_element_type=jnp.float32)
        if near:
            tile = jnp.clip(j - i, -DF_NEAR - 1, DF_NEAR + 1) + DF_NEAR + 1
            s = s + bias_ref[tile, n // 2]
        s_scr[n] = s
        mx_scr[n] = jnp.max(s, axis=0, keepdims=True)

    def consume(j, n, cb):
        kstart = pl.multiple_of(j * DF_TK, DF_TK)
        h = n // 2
        v_aug = jnp.concatenate(
            [v_t_ref[0, h * DIFF_V_DIM:(h + 1) * DIFF_V_DIM, pl.ds(kstart, DF_TK)], ones], axis=0)
        m_prev = m_scr[n]
        m_new = jnp.maximum(m_prev, mx_scr[n] + cb)
        alpha = jnp.exp2(m_prev - m_new)
        p = jnp.exp2(s_scr[n] - (m_new - cb)).astype(jnp.bfloat16)
        acc_scr[n] = alpha * acc_scr[n] + jnp.dot(v_aug, p, preferred_element_type=jnp.float32)
        m_scr[n] = m_new

    def run_chunk(j, j_next, near, next_near, cbs):
        for n in range(n_stat):
            ahead = n + DF_LOOKAHEAD
            if ahead < n_stat:
                issue_scores(j, ahead, near)
            else:
                issue_scores(j_next, ahead - n_stat, next_near)
            consume(j, n, cbs[n // 2])

    for n in range(DF_LOOKAHEAD):
        issue_scores(near_chunk(0), n, True)
    for u in range(n_near):
        last = u == n_near - 1
        run_chunk(near_chunk(u), far_chunk(0) if last else near_chunk(u + 1), True, not last,
                  [0.0] * DIFF_HEADS)

    def far_group(it, carry):
        for u in range(DF_FAR_UNROLL):
            t = DF_FAR_UNROLL * it + u
            j = far_chunk(t)
            j_next = far_chunk(jnp.minimum(t + 1, n_far - 1))
            side = (j > i).astype(jnp.int32)
            run_chunk(j, j_next, False, False, [cfar_ref[side, h] for h in range(DIFF_HEADS)])
        return carry

    lax.fori_loop(0, n_far // DF_FAR_UNROLL, far_group, 0)

    dots = jnp.sum(lq_ref[...] * lk_ref[...], axis=-1, keepdims=True)
    lam = jnp.exp(dots[0:1]) - jnp.exp(dots[1:2]) + lam_init
    outs = []
    for h in range(DIFF_HEADS):
        a1 = acc_scr[2 * h]
        a2 = acc_scr[2 * h + 1]
        o1 = a1[:DIFF_V_DIM] / a1[DIFF_V_DIM:DIFF_V_DIM + 1]
        o2 = a2[:DIFF_V_DIM] / a2[DIFF_V_DIM:DIFF_V_DIM + 1]
        of = o1 - lam * o2
        of = of * lax.rsqrt(jnp.mean(of * of, axis=0, keepdims=True) + LN_EPS)
        outs.append(of * g_ref[...] * (1.0 - lam_init))
    o_ref[0] = jnp.concatenate(outs, axis=0).T.astype(o_ref.dtype)


def _df_attention(q_t, k, v_t, lam_q, lam_k, subln_g, diff_table, lam_init):
    bsz, seq, _ = k.shape
    n_chunks = seq // DF_TK
    assert DF_TQ == DF_TK and DF_LOOKAHEAD < 2 * DIFF_HEADS
    assert (n_chunks - 2 * DF_NEAR - 2) % DF_FAR_UNROLL == 0 and n_chunks > 2 * DF_NEAR + 2
    table_t = diff_table.astype(jnp.float32).T * LOG2E
    bias = _bias_lookup(table_t, _df_bucket_tiles(), "diff_bias")
    cfar = jnp.stack([table_t[:, N_BUCKETS // 2 - 1], table_t[:, N_BUCKETS - 1]])
    n_stat = 2 * DIFF_HEADS
    n_tiles = 2 * DF_NEAR + 3
    const2 = lambda b, i: (0, 0)
    return pl.pallas_call(
        functools.partial(_df_kernel, lam_init=lam_init, n_chunks=n_chunks),
        grid=(bsz, seq // DF_TQ),
        in_specs=[pl.BlockSpec(memory_space=pltpu.SMEM),
                  pl.BlockSpec((1, C_QK_W, DF_TQ), lambda b, i: (b, 0, i)),
                  pl.BlockSpec((1, seq, C_QK_W), lambda b, i: (b, 0, 0)),
                  pl.BlockSpec((1, C_V_W, seq), lambda b, i: (b, 0, 0)),
                  pl.BlockSpec((n_tiles, DIFF_HEADS, DF_TK, DF_TQ), lambda b, i: (0, 0, 0, 0),
                               pipeline_mode=pl.Buffered(1)),
                  pl.BlockSpec((2, DIFF_QK_DIM), const2),
                  pl.BlockSpec((2, DIFF_QK_DIM), const2),
                  pl.BlockSpec((DIFF_V_DIM, 1), const2)],
        out_specs=pl.BlockSpec((1, DF_TQ, C_V_W), lambda b, i: (b, i, 0)),
        out_shape=jax.ShapeDtypeStruct((bsz, seq, C_V_W), jnp.bfloat16),
        scratch_shapes=[pltpu.VMEM((n_stat, DF_QM_ROWS, DF_TQ), jnp.bfloat16),
                        pltpu.VMEM((n_stat, DF_TK, DF_TQ), jnp.float32),
                        pltpu.VMEM((n_stat, 1, DF_TQ), jnp.float32),
                        pltpu.VMEM((n_stat, 1, DF_TQ), jnp.float32),
                        pltpu.VMEM((n_stat, DF_ACC_ROWS, DF_TQ), jnp.float32)],
        compiler_params=_params(("parallel", "arbitrary")),
        name="diff_attention",
    )(cfar, q_t, k, v_t, bias, lam_q.astype(jnp.float32), lam_k.astype(jnp.float32),
      subln_g.astype(jnp.float32).reshape(DIFF_V_DIM, 1))


FF_CHUNK = 1024
MIX_SUBTILE = 256


def _mix_ffn_kernel(x_ref, oa_ref, ob_ref, oc_ref, wo_ref, g1_ref, b1_ref, w1_ref, w2_ref, g2_ref, b2_ref,
                    o_ref, *, alpha):
    subtiles = [slice(r, r + MIX_SUBTILE) for r in range(0, ROW_TILE, MIX_SUBTILE)]

    def mix(rows):
        acc = jnp.dot(oa_ref[rows, :], wo_ref[0:A_W, :], preferred_element_type=jnp.float32)
        acc += jnp.dot(ob_ref[rows, :], wo_ref[A_W:A_W + B_Q_W, :], preferred_element_type=jnp.float32)
        acc += jnp.dot(oc_ref[rows, :], wo_ref[A_W + B_Q_W:MIX_WIDTH, :], preferred_element_type=jnp.float32)
        return alpha * x_ref[rows, :] + acc

    mixed = [mix(rows) for rows in subtiles]
    xs = [_layer_norm_f32(v, g1_ref[...], b1_ref[...]) for v in mixed]
    for rows, x in zip(subtiles, xs):
        xb = x.astype(jnp.bfloat16)
        y = jnp.zeros((MIX_SUBTILE, D_MODEL), jnp.float32)
        for c in range(D_FF // FF_CHUNK):
            sl = slice(c * FF_CHUNK, (c + 1) * FF_CHUNK)
            h = jnp.maximum(jnp.dot(xb, w1_ref[:, sl], preferred_element_type=jnp.float32), 0.0)
            y += jnp.dot((h * h).astype(jnp.bfloat16), w2_ref[sl, :], preferred_element_type=jnp.float32)
        o_ref[rows, :] = _layer_norm_f32(alpha * x + y, g2_ref[...], b2_ref[...])


def _mix_ffn(x2d, oa, ob, oc, wo_bf16, g1, b1, w1_bf16, w2_bf16, g2, b2, alpha):
    n_tok = x2d.shape[0]
    row = lambda i: (i, 0)
    const = lambda i: (0, 0)
    vec = pl.BlockSpec((1, D_MODEL), const)
    resident = functools.partial(pl.BlockSpec, index_map=const, pipeline_mode=pl.Buffered(1))
    return pl.pallas_call(
        functools.partial(_mix_ffn_kernel, alpha=alpha),
        grid=(n_tok // ROW_TILE,),
        in_specs=[pl.BlockSpec((ROW_TILE, D_MODEL), row),
                  pl.BlockSpec((ROW_TILE, A_W), row),
                  pl.BlockSpec((ROW_TILE, B_Q_W), row),
                  pl.BlockSpec((ROW_TILE, C_V_W), row),
                  resident((MIX_WIDTH, D_MODEL)), vec, vec,
                  resident((D_MODEL, D_FF)), resident((D_FF, D_MODEL)), vec, vec],
        out_specs=pl.BlockSpec((ROW_TILE, D_MODEL), row),
        out_shape=jax.ShapeDtypeStruct((n_tok, D_MODEL), jnp.float32),
        compiler_params=_params(("parallel",)),
        name="mix_ffn",
    )(x2d, oa, ob, oc, wo_bf16, g1.reshape(1, D_MODEL), b1.reshape(1, D_MODEL),
      w1_bf16, w2_bf16, g2.reshape(1, D_MODEL), b2.reshape(1, D_MODEL))


def kernel(x, ln_in_g, ln_in_b, t5_table, w_in, w_out, na_rpb, sw_sink, diff_lam_q, diff_lam_k, diff_subln_g,
           ln_mix_g, ln_mix_b, w_ff1, w_ff2, ln_ff_g, ln_ff_b):
    alpha = (2 * DEPTH) ** 0.25
    bsz, seq, _ = x.shape
    assert x.shape[-1] == D_MODEL and seq % max(ROW_TILE, SW_TQ, DF_TQ, DF_TK) == 0
    assert (seq // GRID_W) % NA_ROWS_PER_STEP == 0 and seq // GRID_W >= NA_WIN_ROWS
    sw_table = t5_table[:, :SW_HEADS]
    diff_table = t5_table[:, SW_HEADS:]
    col_scale = np.ones((IN_WIDTH,), np.float32)
    for split in (0, 3):
        col_scale[IN_OFFSETS[split]:IN_OFFSETS[split] + IN_SPLITS[split]] = HEAD_DIM ** -0.5 * LOG2E
    w_in_b = _permute_head_blocks((w_in * col_scale).astype(jnp.bfloat16), 2, IN_OFFSETS[3], _sw_head_order())
    w_out_b = _permute_head_blocks(w_out.astype(jnp.bfloat16), 1, A_W, _sw_head_order())
    w_ff1_b = w_ff1.astype(jnp.bfloat16)
    w_ff2_b = w_ff2.astype(jnp.bfloat16)
    x2d = x.reshape(bsz * seq, D_MODEL)
    for l in range(DEPTH):
        lam_init = 0.8 - 0.6 * math.exp(-0.3 * l)
        if l == 0:
            x2d, *proj = _inproj(x2d, ln_in_g, ln_in_b, w_in_b[l], seq, True)
        else:
            proj = _inproj(x2d, ln_in_g, ln_in_b, w_in_b[l], seq, False)
        *tok_major, qc_t, vc_t = proj
        qa, ka, va, qb, kb, vb, kc = [p.reshape(bsz, seq, -1) for p in tok_major]
        oa = _na_attention(qa, ka, va, na_rpb[l])
        ob = _sw_attention(qb, kb, vb, sw_sink[l], sw_table)
        oc = _df_attention(qc_t, kc, vc_t, diff_lam_q[l], diff_lam_k[l], diff_subln_g[l], diff_table, lam_init)
        x2d = _mix_ffn(x2d, oa.reshape(-1, A_W), ob.reshape(-1, B_Q_W), oc.reshape(-1, C_V_W),
                       w_out_b[l], ln_mix_g[l], ln_mix_b[l], w_ff1_b[l], w_ff2_b[l], ln_ff_g[l], ln_ff_b[l], alpha)
    return x2d.reshape(bsz, seq, D_MODEL)
```
